```python
import functools
import jax, jax.numpy as jnp
from jax import lax
import numpy as np

D_MODEL = 1024
BATCH = 8
SEQ = 2048
DEPTH = 1
DEC_BATCH = 128
DEC_SEQ = 1
PAST_LEN = 16384
PAGE_SIZE = 128

A_HEADS = 4
A_HEAD_DIM = D_MODEL // 16
A_WIDTH = A_HEADS * A_HEAD_DIM
CHUNK = 128
B_HEADS = 4
V_HEAD = D_MODEL // 8
QK_NOPE = V_HEAD
QK_ROPE = 64
B_WIDTH = B_HEADS * V_HEAD
Q_LORA = D_MODEL // 2
KV_LORA = D_MODEL // 4
ROPE_BASE = 10000.0
SM_SCALE = (QK_NOPE + QK_ROPE) ** -0.5
Q_BLOCK = 128
N_MEM = 256
C_HEADS = 4
C_HEAD_DIM = D_MODEL // 16
C_WIDTH = C_HEADS * C_HEAD_DIM
MEM_SCALE = C_HEAD_DIM ** -0.5
EPS = 1e-6
IN_WIDTHS = (A_WIDTH, A_WIDTH, A_WIDTH, Q_LORA, KV_LORA, QK_ROPE, B_WIDTH, C_WIDTH, C_WIDTH)
IN_COLS = 3 * A_WIDTH + Q_LORA + KV_LORA + QK_ROPE + B_WIDTH + 2 * C_WIDTH
MIX_WIDTH = A_WIDTH + B_WIDTH + C_WIDTH

kernel_name = 'hymba_gmlp_mla_memxattn_step'


def _in_offsets():
    offs, acc = [], 0
    for w in IN_WIDTHS[:-1]:
        acc += w
        offs.append(acc)
    return offs


def _rmsnorm(x, g):
    xf = x.astype(jnp.float32)
    y = xf * lax.rsqrt(jnp.mean(xf * xf, axis=-1, keepdims=True) + EPS)
    return (y * g.astype(jnp.float32)).astype(x.dtype)


def _rope(x, pos):
    half = QK_ROPE // 2
    inv = ROPE_BASE ** (-jnp.arange(half, dtype=jnp.float32) / half)
    ang = pos[:, None] * inv[None, :]
    shape = (1, ang.shape[0]) + (1,) * (x.ndim - 3) + (half,)
    cos = jnp.cos(ang).reshape(shape).astype(x.dtype)
    sin = jnp.sin(ang).reshape(shape).astype(x.dtype)
    x1, x2 = x[..., :half], x[..., half:]
    return jnp.concatenate([x1 * cos - x2 * sin, x1 * sin + x2 * cos], axis=-1)


def _chunk_gmlp(u, v, w_s, b_s):
    B, L, H, Dh = v.shape
    n_chunks = -(-L // CHUNK)
    pad = n_chunks * CHUNK - L
    vp = jnp.pad(v, ((0, 0), (0, pad), (0, 0), (0, 0))).reshape(B, n_chunks, CHUNK, H, Dh)
    causal = jnp.tril(jnp.ones((CHUNK, CHUNK), dtype=bool))
    w = jnp.where(causal[None], w_s, jnp.zeros_like(w_s))
    s = jnp.einsum('hts,bnshd->bnthd', w, vp) + jnp.transpose(b_s)[None, None, :, :, None]
    s = s.reshape(B, n_chunks * CHUNK, H, Dh)[:, :L]
    return u * s


def _mla_prompt(q_lat, q_rope, ckv, kr):
    B, L, H, R = q_lat.shape
    nb = L // Q_BLOCK
    qb = q_lat.reshape(B, nb, Q_BLOCK, H, R).transpose(1, 0, 2, 3, 4)
    rb = q_rope.reshape(B, nb, Q_BLOCK, H, QK_ROPE).transpose(1, 0, 2, 3, 4)
    k_pos = jnp.arange(L)

    def block(args):
        q_i, r_i, i = args
        s = jnp.einsum('bqhr,bkr->bhqk', q_i, ckv) + jnp.einsum('bqhe,bke->bhqk', r_i, kr)
        s = s.astype(jnp.float32) * SM_SCALE
        q_pos = i * Q_BLOCK + jnp.arange(Q_BLOCK)
        s = jnp.where(k_pos[None, :] <= q_pos[:, None], s, -jnp.inf)
        p = jax.nn.softmax(s, axis=-1).astype(ckv.dtype)
        return jnp.einsum('bhqk,bkr->bqhr', p, ckv)

    o = lax.map(block, (qb, rb, jnp.arange(nb)))
    return o.transpose(1, 0, 2, 3, 4).reshape(B, L, H, R)


def _mla_sample(q_lat, q_rope, ckv_new, kr_new, pool_ckv, pool_kr, page_table):
    def one(args):
        q_i, r_i, c_new, k_new, pages = args
        c_all = jnp.concatenate([pool_ckv[pages].reshape(-1, KV_LORA), c_new], axis=0)
        k_all = jnp.concatenate([pool_kr[pages].reshape(-1, QK_ROPE), k_new], axis=0)
        past = pages.shape[0] * PAGE_SIZE
        lq = c_new.shape[0]
        s = jnp.einsum('qhr,kr->hqk', q_i, c_all) + jnp.einsum('qhe,ke->hqk', r_i, k_all)
        s = s.astype(jnp.float32) * SM_SCALE
        visible = jnp.arange(past + lq)[None, :] <= past + jnp.arange(lq)[:, None]
        s = jnp.where(visible, s, -jnp.inf)
        p = jax.nn.softmax(s, axis=-1).astype(c_all.dtype)
        return jnp.einsum('hqk,kr->qhr', p, c_all)

    return lax.map(one, (q_lat, q_rope, ckv_new, kr_new, page_table))


def _mem_kv(mem, g_mem, w_mem_k, w_mem_v):
    mn = _rmsnorm(mem, g_mem)
    return (jnp.einsum('bmd,dhe->bmhe', mn, w_mem_k), jnp.einsum('bmd,dhe->bmhe', mn, w_mem_v))


def _mem_attn(q, mem_k, mem_v):
    s = jnp.einsum('blhe,bmhe->bhlm', q, mem_k).astype(jnp.float32) * MEM_SCALE
    p = jax.nn.softmax(s, axis=-1).astype(q.dtype)
    return jnp.einsum('bhlm,bmhe->blhe', p, mem_v)


def _layer(x, pos, attend, mem_k, mem_v, g_pre, w_in, g_av, w_s, b_s, g_q, w_uq, g_kv, w_uk, w_uv, w_out, g_post):
    B, L, _ = x.shape
    h = _rmsnorm(x, g_pre)
    z = jnp.einsum('bld,dc->blc', h, w_in)
    a_u, a_v, a_g, b_cq, b_ckv, b_kr, b_g, c_q, c_g = jnp.split(z, _in_offsets(), axis=-1)
    a_u = a_u.reshape(B, L, A_HEADS, A_HEAD_DIM)
    a_v = _rmsnorm(a_v.reshape(B, L, A_HEADS, A_HEAD_DIM), g_av)
    a_out = _chunk_gmlp(a_u, a_v, w_s, b_s).reshape(B, L, A_WIDTH) * jax.nn.silu(a_g)
    q = jnp.einsum('blc,chd->blhd', _rmsnorm(b_cq, g_q), w_uq)
    q_nope = q[..., :QK_NOPE]
    q_rope = _rope(q[..., QK_NOPE:], pos)
    q_lat = jnp.einsum('blhd,rhd->blhr', q_nope, w_uk)
    ckv = _rmsnorm(b_ckv, g_kv)
    kr = _rope(b_kr, pos)
    o_lat = attend(q_lat, q_rope, ckv, kr)
    b_out = jnp.einsum('blhr,rhd->blhd', o_lat, w_uv).reshape(B, L, B_WIDTH) * jax.nn.silu(b_g)
    c_out = _mem_attn(c_q.reshape(B, L, C_HEADS, C_HEAD_DIM), mem_k, mem_v).reshape(B, L, C_WIDTH) * jax.nn.silu(c_g)
    mix = jnp.einsum('blc,cd->bld', jnp.concatenate([a_out, b_out, c_out], axis=-1), w_out)
    return x + _rmsnorm(mix, g_post), ckv, kr, a_v


def setup_inputs(seed: int = 0) -> dict:
    key = jax.random.key(seed)
    ks = jax.random.split(key, 32)
    f32 = jnp.float32
    n_pages = PAST_LEN // PAGE_SIZE
    n_used = DEC_BATCH * n_pages
    pool = n_used + n_used // 4

    def nrm(k, shape, scale):
        return jax.random.normal(k, shape, f32) * scale

    def gain(k, shape):
        return 1.0 + 0.05 * jax.random.normal(k, shape, f32)

    page_table = jax.random.permutation(ks[0], pool)[:n_used].reshape(DEC_BATCH, n_pages).astype(jnp.int32)
    return {
        'x_prompt': nrm(ks[1], (BATCH, SEQ, D_MODEL), 1.0),
        'x_sample': nrm(ks[2], (DEC_BATCH, DEC_SEQ, D_MODEL), 1.0),
        'mem_prompt': nrm(ks[3], (BATCH, N_MEM, D_MODEL), 1.0),
        'cache_ckv': nrm(ks[4], (DEPTH, pool, PAGE_SIZE, KV_LORA), 1.0),
        'cache_krope': nrm(ks[5], (DEPTH, pool, PAGE_SIZE, QK_ROPE), 1.0),
        'cache_mem_k': nrm(ks[6], (DEPTH, DEC_BATCH, N_MEM, C_HEADS, C_HEAD_DIM), 1.0),
        'cache_mem_v': nrm(ks[7], (DEPTH, DEC_BATCH, N_MEM, C_HEADS, C_HEAD_DIM), 1.0),
        'page_table': page_table,
        'g_pre': gain(ks[8], (DEPTH, D_MODEL)),
        'w_in': nrm(ks[9], (DEPTH, D_MODEL, IN_COLS), D_MODEL ** -0.5),
        'g_av': gain(ks[10], (DEPTH, A_HEADS, A_HEAD_DIM)),
        'w_s': nrm(ks[11], (DEPTH, A_HEADS, CHUNK, CHUNK), CHUNK ** -0.5),
        'b_s': 1.0 + 0.1 * jax.random.normal(ks[12], (DEPTH, A_HEADS, CHUNK), f32),
        'g_q': gain(ks[13], (DEPTH, Q_LORA)),
        'w_uq': nrm(ks[14], (DEPTH, Q_LORA, B_HEADS, QK_NOPE + QK_ROPE), Q_LORA ** -0.5),
        'g_kv': gain(ks[15], (DEPTH, KV_LORA)),
        'w_uk': nrm(ks[16], (DEPTH, KV_LORA, B_HEADS, QK_NOPE), KV_LORA ** -0.5),
        'w_uv': nrm(ks[17], (DEPTH, KV_LORA, B_HEADS, V_HEAD), KV_LORA ** -0.5),
        'g_mem': gain(ks[18], (DEPTH, D_MODEL)),
        'w_mem_k': nrm(ks[19], (DEPTH, D_MODEL, C_HEADS, C_HEAD_DIM), D_MODEL ** -0.5),
        'w_mem_v': nrm(ks[20], (DEPTH, D_MODEL, C_HEADS, C_HEAD_DIM), D_MODEL ** -0.5),
        'w_out': nrm(ks[21], (DEPTH, MIX_WIDTH, D_MODEL), MIX_WIDTH ** -0.5),
        'g_post': gain(ks[22], (DEPTH, D_MODEL)),
    }


def reference(x_prompt, x_sample, mem_prompt, cache_ckv, cache_krope, cache_mem_k, cache_mem_v, page_table,
              g_pre, w_in, g_av, w_s, b_s, g_q, w_uq, g_kv, w_uk, w_uv, g_mem, w_mem_k, w_mem_v, w_out, g_post):
    past_len = page_table.shape[1] * PAGE_SIZE
    bp, lp, _ = x_prompt.shape
    pos_p = jnp.arange(lp, dtype=jnp.float32)
    pos_s = past_len + jnp.arange(x_sample.shape[1], dtype=jnp.float32)
    y_p, y_s = x_prompt, x_sample
    ckv_p_l, kr_p_l, mk_p_l, mv_p_l, ckv_s_l, kr_s_l, av_s_l = [], [], [], [], [], [], []
    for l in range(DEPTH):
        shared = (g_pre[l], w_in[l], g_av[l], w_s[l], b_s[l], g_q[l], w_uq[l], g_kv[l], w_uk[l], w_uv[l], w_out[l], g_post[l])
        mk, mv = _mem_kv(mem_prompt, g_mem[l], w_mem_k[l], w_mem_v[l])
        y_p, ckv_p, kr_p, _ = _layer(y_p, pos_p, _mla_prompt, mk, mv, *shared)
        ckv_p_l.append(ckv_p.reshape(bp, lp // PAGE_SIZE, PAGE_SIZE, KV_LORA))
        kr_p_l.append(kr_p.reshape(bp, lp // PAGE_SIZE, PAGE_SIZE, QK_ROPE))
        mk_p_l.append(mk)
        mv_p_l.append(mv)
        attend_s = functools.partial(_mla_sample, pool_ckv=cache_ckv[l], pool_kr=cache_krope[l], page_table=page_table)
        y_s, ckv_s, kr_s, av_s = _layer(y_s, pos_s, attend_s, cache_mem_k[l], cache_mem_v[l], *shared)
        ckv_s_l.append(ckv_s)
        kr_s_l.append(kr_s)
        av_s_l.append(av_s)
    new_ckv_prompt = jnp.stack(ckv_p_l, 0)
    new_krope_prompt = jnp.stack(kr_p_l, 0)
    new_mem_k_prompt = jnp.stack(mk_p_l, 0)
    new_mem_v_prompt = jnp.stack(mv_p_l, 0)
    new_ckv_sample = jnp.stack(ckv_s_l, 0)
    new_krope_sample = jnp.stack(kr_s_l, 0)
    new_gmlp_v_sample = jnp.stack(av_s_l, 0)
    return (y_p, y_s, new_ckv_prompt, new_krope_prompt, new_mem_k_prompt, new_mem_v_prompt, new_ckv_sample, new_krope_sample, new_gmlp_v_sample)
```

```python
import functools

import jax
import jax.numpy as jnp
from jax import lax
from jax.experimental import pallas as pl
from jax.experimental.pallas import tpu as pltpu

F32 = jnp.float32
BF16 = jnp.bfloat16

LANES = 128
SUBLANES = 8
VMEM_BYTES_V7X = 64 * 1024 * 1024

D_MODEL = 1024
PAGE = 128
A_HEADS, A_HD = 4, 64
A_W = A_HEADS * A_HD
CHUNK = 128
B_HEADS = 4
V_HEAD = 128
QK_NOPE = 128
QK_ROPE = 64
HALF = QK_ROPE // 2
B_W = B_HEADS * V_HEAD
Q_LORA = 512
KV_LORA = 256
ROPE_BASE = 10000.0
SM_SCALE = (QK_NOPE + QK_ROPE) ** -0.5
N_MEM = 256
C_HEADS, C_HD = 4, 64
C_W = C_HEADS * C_HD
MEM_SCALE = C_HD ** -0.5
EPS = 1e-6
NEG_INF = float("-inf")

OFF_AU = 0
OFF_AV = OFF_AU + A_W
OFF_AG = OFF_AV + A_W
OFF_CQ = OFF_AG + A_W
OFF_CKV = OFF_CQ + Q_LORA
OFF_KR = OFF_CKV + KV_LORA
OFF_BG = OFF_KR + 2 * QK_ROPE
OFF_CQC = OFF_BG + B_W
OFF_CG = OFF_CQC + C_W
IN_EXT = OFF_CG + C_W
Q_HEAD_W = QK_NOPE + 2 * QK_ROPE
Q_EXT = B_HEADS * Q_HEAD_W

TQ = 256
DEC_ROWS = 16
DEC_CP = 16
DEC_NBUF = 4
TAIL_SEQ = 8


def _dot(a, b):
    return jnp.dot(a, b, preferred_element_type=F32)


def _dot_nt(a, b):
    return lax.dot_general(a, b, (((1,), (1,)), ((), ())), preferred_element_type=F32)


def _rmsnorm(x, g):
    return (x * lax.rsqrt(jnp.mean(x * x, axis=-1, keepdims=True) + EPS)) * g


def _silu(x):
    return x * (1.0 / (1.0 + jnp.exp(-x)))


def _lane_head(shape, width):
    assert width & (width - 1) == 0
    return lax.shift_right_logical(lax.broadcasted_iota(jnp.int32, shape, len(shape) - 1), width.bit_length() - 1)


def _group_rmsnorm(v, ones_blk, g):
    sq = v * v
    hi = sq.astype(BF16)
    lo = (sq - hi.astype(F32)).astype(BF16)
    ssum = _dot(hi, ones_blk) + _dot(lo, ones_blk)
    return (v * lax.rsqrt(ssum * (1.0 / A_HD) + EPS)) * g


def _rope128(v, cs):
    w = v * cs
    return w + pltpu.roll(w, QK_ROPE, 1)


def _project(x, cs, g_pre, w_in, ones_blk, g_av, g_q, w_uq, g_kv):
    h = _rmsnorm(x, g_pre).astype(BF16)
    z = _dot(h, w_in)
    a_v = _group_rmsnorm(z[:, OFF_AV:OFF_AV + A_W], ones_blk, g_av)
    cq = _rmsnorm(z[:, OFF_CQ:OFF_CQ + Q_LORA], g_q).astype(BF16)
    q = _dot(cq, w_uq)
    ckv = _rmsnorm(z[:, OFF_CKV:OFF_CKV + KV_LORA], g_kv)
    kr = _rope128(z[:, OFF_KR:OFF_KR + 2 * QK_ROPE], cs)
    return z, a_v, q, ckv, kr


def _softmax_lanes(s):
    m = jnp.max(s, axis=-1, keepdims=True)
    p = jnp.exp(s - m)
    return p / jnp.sum(p, axis=-1, keepdims=True)


def _mem_attn_heads(cq, mk, mv):
    head = _lane_head(cq.shape, C_HD)
    out = jnp.zeros(cq.shape, F32)
    for hd in range(C_HEADS):
        qm = jnp.where(head == hd, cq, 0.0).astype(BF16)
        p = _softmax_lanes(_dot_nt(qm, mk) * MEM_SCALE)
        o = _dot(p.astype(BF16), mv)
        out = out + jnp.where(head == hd, o, 0.0)
    return out


def _finish(x, mix, w_out, g_post):
    return x + _rmsnorm(_dot(mix, w_out), g_post)


def _memkv_kernel(mem_ref, g_ref, w_ref, mk_ref, mv_ref):
    mn = _rmsnorm(mem_ref[...], g_ref[...]).astype(BF16)
    kv = _dot(mn, w_ref[...])
    mk_ref[...] = kv[:, :C_W]
    mv_ref[...] = kv[:, C_W:]


def _memkv_call(mem, g_mem, w_memkv):
    nb = mem.shape[0]
    return pl.pallas_call(
        _memkv_kernel,
        grid=(nb,),
        in_specs=[
            pl.BlockSpec((None, N_MEM, D_MODEL), lambda b: (b, 0, 0)),
            pl.BlockSpec((1, D_MODEL), lambda b: (0, 0)),
            pl.BlockSpec((D_MODEL, 2 * C_W), lambda b: (0, 0)),
        ],
        out_specs=[
            pl.BlockSpec((None, N_MEM, C_W), lambda b: (b, 0, 0)),
            pl.BlockSpec((None, N_MEM, C_W), lambda b: (b, 0, 0)),
        ],
        out_shape=[jax.ShapeDtypeStruct((nb, N_MEM, C_W), F32)] * 2,
        compiler_params=pltpu.CompilerParams(dimension_semantics=("arbitrary",)),
        name="memkv",
    )(mem, g_mem, w_memkv)


def _prompt_kernel(x_ref, cs_ref, mk_ref, mv_ref, g_pre_ref, w_in_ref, ones_ref, g_av_ref, w_s_ref,
                   b_s_ref, g_q_ref, w_uq_ref, g_kv_ref, w_ukv_ref, w_out_ref, g_post_ref,
                   y_ref, ckv_ref, kr_ref, k_scr, v_scr, mix_scr):
    i = pl.program_id(1)
    x = x_ref[...]
    cs = cs_ref[...]
    z, a_v, q, ckv, kr = _project(x, cs, g_pre_ref[...], w_in_ref[...], ones_ref[...], g_av_ref[...],
                                  g_q_ref[...], w_uq_ref[...], g_kv_ref[...])
    ckv_ref[...] = ckv
    kr_ref[...] = kr[:, :QK_ROPE]

    tril = (lax.broadcasted_iota(jnp.int32, (CHUNK, CHUNK), 1)
            <= lax.broadcasted_iota(jnp.int32, (CHUNK, CHUNK), 0))
    w_stack = jnp.concatenate(
        [jnp.where(tril, w_s_ref[hd], 0.0).astype(BF16) for hd in range(A_HEADS)], axis=0)
    head_a = _lane_head((CHUNK, A_W), A_HD)
    for c in range(TQ // CHUNK):
        rows = slice(c * CHUNK, (c + 1) * CHUNK)
        r = _dot(w_stack, a_v[rows].astype(BF16))
        s = b_s_ref[...]
        for hd in range(A_HEADS):
            s = s + jnp.where(head_a == hd, r[hd * CHUNK:(hd + 1) * CHUNK], 0.0)
        a_out = (z[rows, OFF_AU:OFF_AU + A_W] * s) * _silu(z[rows, OFF_AG:OFF_AG + A_W])
        mix_scr[rows, 0:A_W] = a_out.astype(BF16)

    kv = _dot(ckv.astype(BF16), w_ukv_ref[...])
    lane = lax.broadcasted_iota(jnp.int32, (TQ, 2 * QK_ROPE), 1)
    kr_pad = jnp.where(lane < QK_ROPE, kr, 0.0).astype(BF16)
    row0 = pl.multiple_of(i * TQ, TQ)
    for hd in range(B_HEADS):
        k_scr[hd, pl.ds(row0, TQ), 0:QK_NOPE] = kv[:, hd * QK_NOPE:(hd + 1) * QK_NOPE].astype(BF16)
        k_scr[hd, pl.ds(row0, TQ), QK_NOPE:] = kr_pad
        v_scr[hd, pl.ds(row0, TQ), :] = kv[:, B_W + hd * V_HEAD:B_W + (hd + 1) * V_HEAD].astype(BF16)

    diag = (lax.broadcasted_iota(jnp.int32, (TQ, TQ), 1) <= lax.broadcasted_iota(jnp.int32, (TQ, TQ), 0))
    for hd in range(B_HEADS):
        qh = q[:, hd * Q_HEAD_W:(hd + 1) * Q_HEAD_W]
        q_rope = _rope128(qh[:, QK_NOPE:], cs)
        q_h = jnp.concatenate([qh[:, :QK_NOPE], q_rope], axis=-1).astype(BF16)

        def step(s, v_blk, carry):
            m, l, acc = carry
            m_new = jnp.maximum(m, jnp.max(s, axis=-1, keepdims=True))
            alpha = jnp.exp(m - m_new)
            p = jnp.exp(s - m_new)
            l = alpha * l + jnp.sum(p, axis=-1, keepdims=True)
            acc = alpha * acc + _dot(p.astype(BF16), v_blk)
            return m_new, l, acc

        def body(j, carry, hd=hd, q_h=q_h):
            r0 = pl.multiple_of(j * TQ, TQ)
            s = _dot_nt(q_h, k_scr[hd, pl.ds(r0, TQ), :]) * SM_SCALE
            return step(s, v_scr[hd, pl.ds(r0, TQ), :], carry)

        carry = (jnp.full((TQ, 1), NEG_INF, F32), jnp.zeros((TQ, 1), F32), jnp.zeros((TQ, V_HEAD), F32))
        carry = lax.fori_loop(0, i, body, carry)
        s = _dot_nt(q_h, k_scr[hd, pl.ds(row0, TQ), :]) * SM_SCALE
        _, l, acc = step(jnp.where(diag, s, NEG_INF), v_scr[hd, pl.ds(row0, TQ), :], carry)
        b_g = z[:, OFF_BG + hd * V_HEAD:OFF_BG + (hd + 1) * V_HEAD]
        mix_scr[:, A_W + hd * V_HEAD:A_W + (hd + 1) * V_HEAD] = ((acc / l) * _silu(b_g)).astype(BF16)

    c_out = _mem_attn_heads(z[:, OFF_CQC:OFF_CQC + C_W], mk_ref[...].astype(BF16), mv_ref[...].astype(BF16))
    mix_scr[:, A_W + B_W:] = (c_out * _silu(z[:, OFF_CG:OFF_CG + C_W])).astype(BF16)

    y_ref[...] = _finish(x, mix_scr[...], w_out_ref[...], g_post_ref[...])


def _full(shape):
    return pl.BlockSpec(shape, lambda *_: (0,) * len(shape))


def _prompt_call(x, cs, mk, mv, g_pre, w_in, ones_blk, g_av, w_s, b_s, g_q, w_uq, g_kv, w_ukv, w_out, g_post):
    nb, seq, _ = x.shape
    nq = seq // TQ
    weights = (g_pre, w_in, ones_blk, g_av, w_s, b_s, g_q, w_uq, g_kv, w_ukv, w_out, g_post)
    return pl.pallas_call(
        _prompt_kernel,
        grid=(nb, nq),
        in_specs=[
            pl.BlockSpec((None, TQ, D_MODEL), lambda b, i: (b, i, 0)),
            pl.BlockSpec((TQ, 2 * QK_ROPE), lambda b, i: (i, 0)),
            pl.BlockSpec((None, N_MEM, C_W), lambda b, i: (b, 0, 0)),
            pl.BlockSpec((None, N_MEM, C_W), lambda b, i: (b, 0, 0)),
        ] + [_full(w.shape) for w in weights],
        out_specs=[
            pl.BlockSpec((None, TQ, D_MODEL), lambda b, i: (b, i, 0)),
            pl.BlockSpec((None, TQ, KV_LORA), lambda b, i: (b, i, 0)),
            pl.BlockSpec((None, TQ, QK_ROPE), lambda b, i: (b, i, 0)),
        ],
        out_shape=[
            jax.ShapeDtypeStruct((nb, seq, D_MODEL), F32),
            jax.ShapeDtypeStruct((nb, seq, KV_LORA), F32),
            jax.ShapeDtypeStruct((nb, seq, QK_ROPE), F32),
        ],
        scratch_shapes=[
            pltpu.VMEM((B_HEADS, seq, QK_NOPE + 2 * QK_ROPE), BF16),
            pltpu.VMEM((B_HEADS, seq, V_HEAD), BF16),
            pltpu.VMEM((TQ, D_MODEL), BF16),
        ],
        compiler_params=pltpu.CompilerParams(
            dimension_semantics=("arbitrary", "arbitrary"),
            vmem_limit_bytes=VMEM_BYTES_V7X * 3 // 4),
        name="prompt_layer",
    )(x, cs, mk, mv, *weights)


def _sample_proj_kernel(x_ref, cs_ref, g_pre_ref, w_in_ref, ones_ref, g_av_ref, w00_ref, b0_ref,
                        g_q_ref, w_uq_ref, g_kv_ref, w_ukt_ref,
                        z_ref, av_ref, aout_ref, qlat_ref, qrope_ref, ckv_ref, kr_ref):
    cs = cs_ref[...]
    z, a_v, q, ckv, kr = _project(x_ref[...], cs, g_pre_ref[...], w_in_ref[...], ones_ref[...],
                                  g_av_ref[...], g_q_ref[...], w_uq_ref[...], g_kv_ref[...])
    z_ref[...] = z
    av_ref[...] = a_v
    ckv_ref[...] = ckv
    kr_ref[...] = kr[:, :QK_ROPE]
    s = w00_ref[...] * a_v + b0_ref[...]
    aout_ref[...] = (z[:, OFF_AU:OFF_AU + A_W] * s) * _silu(z[:, OFF_AG:OFF_AG + A_W])
    for hd in range(B_HEADS):
        qh = q[:, hd * Q_HEAD_W:(hd + 1) * Q_HEAD_W]
        qlat_ref[:, hd * KV_LORA:(hd + 1) * KV_LORA] = _dot(qh[:, :QK_NOPE].astype(BF16), w_ukt_ref[hd])
        qrope_ref[:, hd * 2 * QK_ROPE:(hd + 1) * 2 * QK_ROPE] = _rope128(qh[:, QK_NOPE:], cs)


def _sample_proj_call(x, cs, g_pre, w_in, ones_blk, g_av, w00, b0, g_q, w_uq, g_kv, w_ukt):
    n = x.shape[0]
    args = (x, cs, g_pre, w_in, ones_blk, g_av, w00, b0, g_q, w_uq, g_kv, w_ukt)
    widths = (IN_EXT, A_W, A_W, B_HEADS * KV_LORA, B_HEADS * 2 * QK_ROPE, KV_LORA, QK_ROPE)
    return pl.pallas_call(
        _sample_proj_kernel,
        grid=(1,),
        in_specs=[_full(a.shape) for a in args],
        out_specs=[_full((n, w)) for w in widths],
        out_shape=[jax.ShapeDtypeStruct((n, w), F32) for w in widths],
        compiler_params=pltpu.CompilerParams(dimension_semantics=("arbitrary",)),
        name="sample_proj",
    )(*args)


def _decode_kernel(pt_ref, qlat_ref, qrope_ref, cnew_ref, knew_ref, ckv_hbm, kr_hbm, o_ref,
                   cbuf, kbuf, sems, *, n_seq, n_chunks):
    b = pl.program_id(0)

    def page_copies(seq, chunk, slot):
        out = []
        for k in range(DEC_CP):
            page = pt_ref[seq, chunk * DEC_CP + k]
            out.append(pltpu.make_async_copy(ckv_hbm.at[page], cbuf.at[slot, k], sems.at[0, slot]))
            out.append(pltpu.make_async_copy(kr_hbm.at[page], kbuf.at[slot, k], sems.at[1, slot]))
        return out

    def start(seq, chunk, slot):
        for cp in page_copies(seq, chunk, slot):
            cp.start()

    @pl.when(b == 0)
    def _():
        for c in range(DEC_NBUF - 1):
            start(0, c, c)

    q_lat = qlat_ref[...]
    q_rope = qrope_ref[...]
    c_new = cnew_ref[...]
    k_new = knew_ref[...]
    s_new = (jnp.sum(q_lat * c_new, axis=-1, keepdims=True)
             + jnp.sum(q_rope * k_new, axis=-1, keepdims=True)) * SM_SCALE
    m = s_new
    l = jnp.ones((DEC_ROWS, 1), F32)
    acc = jnp.broadcast_to(c_new, (DEC_ROWS, KV_LORA))
    q_lat = q_lat.astype(BF16)
    q_rope = q_rope.astype(BF16)

    for c in range(n_chunks):
        ahead = c + DEC_NBUF - 1
        slot_ahead = ahead % DEC_NBUF
        if ahead < n_chunks:
            start(b, ahead, slot_ahead)
        else:
            @pl.when(b + 1 < n_seq)
            def _(ahead=ahead, slot_ahead=slot_ahead):
                start(b + 1, ahead - n_chunks, slot_ahead)
        slot = c % DEC_NBUF
        for cp in page_copies(b, c, slot):
            cp.wait()
        ckv_c = cbuf[slot].reshape(DEC_CP * PAGE, KV_LORA).astype(BF16)
        kr_c = kbuf[slot].reshape(DEC_CP * PAGE, QK_ROPE).astype(BF16)
        s = (_dot_nt(q_lat, ckv_c) + _dot_nt(q_rope, kr_c)) * SM_SCALE
        m_new = jnp.maximum(m, jnp.max(s, axis=-1, keepdims=True))
        alpha = jnp.exp(m - m_new)
        p = jnp.exp(s - m_new)
        l = alpha * l + jnp.sum(p, axis=-1, keepdims=True)
        acc = alpha * acc + _dot(p.astype(BF16), ckv_c)
        m = m_new
    o_ref[...] = acc / l


def _decode_call(page_table, q_lat, q_rope, c_new, k_new, pool_ckv, pool_kr):
    n_seq, n_pages = page_table.shape
    assert n_pages % DEC_CP == 0
    n_chunks = n_pages // DEC_CP
    assert n_chunks % DEC_NBUF == 0 and n_chunks >= DEC_NBUF
    grid_spec = pltpu.PrefetchScalarGridSpec(
        num_scalar_prefetch=1,
        grid=(n_seq,),
        in_specs=[
            pl.BlockSpec((None, DEC_ROWS, KV_LORA), lambda b, pt: (b, 0, 0)),
            pl.BlockSpec((None, DEC_ROWS, QK_ROPE), lambda b, pt: (b, 0, 0)),
            pl.BlockSpec((None, 1, KV_LORA), lambda b, pt: (b, 0, 0)),
            pl.BlockSpec((None, 1, QK_ROPE), lambda b, pt: (b, 0, 0)),
            pl.BlockSpec(memory_space=pl.ANY),
            pl.BlockSpec(memory_space=pl.ANY),
        ],
        out_specs=pl.BlockSpec((None, DEC_ROWS, KV_LORA), lambda b, pt: (b, 0, 0)),
        scratch_shapes=[
            pltpu.VMEM((DEC_NBUF, DEC_CP, PAGE, KV_LORA), F32),
            pltpu.VMEM((DEC_NBUF, DEC_CP, PAGE, QK_ROPE), F32),
            pltpu.SemaphoreType.DMA((2, DEC_NBUF)),
        ],
    )
    return pl.pallas_call(
        functools.partial(_decode_kernel, n_seq=n_seq, n_chunks=n_chunks),
        grid_spec=grid_spec,
        out_shape=jax.ShapeDtypeStruct((n_seq, DEC_ROWS, KV_LORA), F32),
        compiler_params=pltpu.CompilerParams(
            dimension_semantics=("arbitrary",),
            vmem_limit_bytes=VMEM_BYTES_V7X * 3 // 4),
        name="paged_decode",
    )(page_table, q_lat, q_rope, c_new, k_new, pool_ckv, pool_kr)


def _sample_tail_kernel(x_ref, z_ref, aout_ref, olat_ref, mk_ref, mv_ref, w_uv_ref, w_out_ref, g_post_ref,
                        y_ref, mix_scr, c_scr):
    z = z_ref[...]
    mix_scr[:, 0:A_W] = aout_ref[...].astype(BF16)
    for hd in range(B_HEADS):
        o = _dot(olat_ref[hd].astype(BF16), w_uv_ref[hd])
        b_g = z[:, OFF_BG + hd * V_HEAD:OFF_BG + (hd + 1) * V_HEAD]
        mix_scr[:, A_W + hd * V_HEAD:A_W + (hd + 1) * V_HEAD] = (o * _silu(b_g)).astype(BF16)
    row = lax.broadcasted_iota(jnp.int32, (SUBLANES, C_W), 0)
    own = _lane_head((SUBLANES, C_W), C_HD) == row
    for n in range(TAIL_SEQ):
        cq = z[n:n + 1, OFF_CQC:OFF_CQC + C_W]
        qm = jnp.where(own, jnp.broadcast_to(cq, (SUBLANES, C_W)), 0.0).astype(BF16)
        p = _softmax_lanes(_dot_nt(qm, mk_ref[n].astype(BF16)) * MEM_SCALE)
        o = _dot(p.astype(BF16), mv_ref[n].astype(BF16))
        c_scr[n:n + 1, :] = jnp.sum(jnp.where(own, o, 0.0), axis=0, keepdims=True)
    mix_scr[:, A_W + B_W:] = (c_scr[...] * _silu(z[:, OFF_CG:OFF_CG + C_W])).astype(BF16)
    y_ref[...] = _finish(x_ref[...], mix_scr[...], w_out_ref[...], g_post_ref[...])


def _sample_tail_call(x, z, a_out, o_lat, mem_k, mem_v, w_uv, w_out, g_post):
    n = x.shape[0]
    assert n % TAIL_SEQ == 0
    return pl.pallas_call(
        _sample_tail_kernel,
        grid=(n // TAIL_SEQ,),
        in_specs=[
            pl.BlockSpec((TAIL_SEQ, D_MODEL), lambda g: (g, 0)),
            pl.BlockSpec((TAIL_SEQ, IN_EXT), lambda g: (g, 0)),
            pl.BlockSpec((TAIL_SEQ, A_W), lambda g: (g, 0)),
            pl.BlockSpec((B_HEADS, TAIL_SEQ, KV_LORA), lambda g: (0, g, 0)),
            pl.BlockSpec((TAIL_SEQ, N_MEM, C_W), lambda g: (g, 0, 0)),
            pl.BlockSpec((TAIL_SEQ, N_MEM, C_W), lambda g: (g, 0, 0)),
            _full(w_uv.shape), _full(w_out.shape), _full(g_post.shape),
        ],
        out_specs=pl.BlockSpec((TAIL_SEQ, D_MODEL), lambda g: (g, 0)),
        out_shape=jax.ShapeDtypeStruct((n, D_MODEL), F32),
        scratch_shapes=[pltpu.VMEM((TAIL_SEQ, D_MODEL), BF16), pltpu.VMEM((TAIL_SEQ, C_W), F32)],
        compiler_params=pltpu.CompilerParams(dimension_semantics=("arbitrary",)),
        name="sample_tail",
    )(x, z, a_out, o_lat, mem_k, mem_v, w_uv, w_out, g_post)


def _swap_halves(w):
    return jnp.concatenate([w[..., HALF:], w[..., :HALF]], axis=-1)


def _rope_table(pos):
    inv = ROPE_BASE ** (-jnp.arange(HALF, dtype=F32) / HALF)
    ang = pos[:, None] * inv[None, :]
    cos, sin = jnp.cos(ang), jnp.sin(ang)
    return jnp.concatenate([cos, cos, -sin, sin], axis=-1)


def kernel(x_prompt, x_sample, mem_prompt, cache_ckv, cache_krope, cache_mem_k, cache_mem_v, page_table,
           g_pre, w_in, g_av, w_s, b_s, g_q, w_uq, g_kv, w_uk, w_uv, g_mem, w_mem_k, w_mem_v, w_out, g_post):
    depth = g_pre.shape[0]
    assert depth == 1
    nb, seq, _ = x_prompt.shape
    n_seq, dec_seq, _ = x_sample.shape
    assert dec_seq == 1 and seq % TQ == 0
    past_len = page_table.shape[1] * PAGE
    l = 0

    row = lambda g: g.reshape(1, -1)
    kr_cols = w_in[l][:, OFF_KR:OFF_KR + QK_ROPE]
    w_in_ext = jnp.concatenate(
        [w_in[l][:, :OFF_KR + QK_ROPE], _swap_halves(kr_cols), w_in[l][:, OFF_KR + QK_ROPE:]], axis=1).astype(BF16)
    w_uq_l = w_uq[l]
    w_uq_ext = jnp.concatenate(
        [w_uq_l, _swap_halves(w_uq_l[..., QK_NOPE:])], axis=-1).reshape(Q_LORA, Q_EXT).astype(BF16)
    w_ukv = jnp.concatenate([w_uk[l].reshape(KV_LORA, B_W), w_uv[l].reshape(KV_LORA, B_W)], axis=1).astype(BF16)
    w_ukt = jnp.transpose(w_uk[l], (1, 2, 0)).astype(BF16)
    w_uv_h = jnp.transpose(w_uv[l], (1, 0, 2)).astype(BF16)
    w_out_b = w_out[l].astype(BF16)
    w_memkv = jnp.concatenate(
        [w_mem_k[l].reshape(D_MODEL, C_W), w_mem_v[l].reshape(D_MODEL, C_W)], axis=1).astype(BF16)
    ones_blk = (jnp.arange(A_W)[:, None] // A_HD == jnp.arange(A_W)[None, :] // A_HD).astype(BF16)
    g_av_l = g_av[l].reshape(1, A_W)
    b_s_lanes = jnp.repeat(jnp.transpose(b_s[l]), A_HD, axis=1)
    w00 = jnp.repeat(w_s[l][:, 0, 0], A_HD).reshape(1, A_W)
    b0 = b_s_lanes[0:1]

    cs_p = _rope_table(jnp.arange(seq, dtype=F32))
    cs_s = jnp.broadcast_to(_rope_table(past_len + jnp.arange(dec_seq, dtype=F32)), (n_seq, 2 * QK_ROPE))

    mk, mv = _memkv_call(mem_prompt, row(g_mem[l]), w_memkv)
    y_p, ckv_p, kr_p = _prompt_call(
        x_prompt, cs_p, mk, mv, row(g_pre[l]), w_in_ext, ones_blk, g_av_l, w_s[l], b_s_lanes,
        row(g_q[l]), w_uq_ext, row(g_kv[l]), w_ukv, w_out_b, row(g_post[l]))

    x_s = x_sample.reshape(n_seq, D_MODEL)
    z_s, av_s, aout_s, qlat_s, qrope_s, ckv_s, kr_s = _sample_proj_call(
        x_s, cs_s, row(g_pre[l]), w_in_ext, ones_blk, g_av_l, w00, b0, row(g_q[l]), w_uq_ext, row(g_kv[l]), w_ukt)
    pad_rows = ((0, 0), (0, DEC_ROWS - B_HEADS), (0, 0))
    q_lat = jnp.pad(qlat_s.reshape(n_seq, B_HEADS, KV_LORA), pad_rows)
    q_rope = jnp.pad(qrope_s.reshape(n_seq, B_HEADS, 2 * QK_ROPE)[..., :QK_ROPE], pad_rows)
    o_lat = _decode_call(page_table, q_lat, q_rope, ckv_s.reshape(n_seq, 1, KV_LORA),
                         kr_s.reshape(n_seq, 1, QK_ROPE), cache_ckv[l], cache_krope[l])
    o_lat_h = jnp.transpose(o_lat[:, :B_HEADS], (1, 0, 2))
    y_s = _sample_tail_call(
        x_s, z_s, aout_s, o_lat_h, cache_mem_k[l].reshape(n_seq, N_MEM, C_W),
        cache_mem_v[l].reshape(n_seq, N_MEM, C_W), w_uv_h, w_out_b, row(g_post[l]))

    n_pg = seq // PAGE
    return (
        y_p,
        y_s.reshape(n_seq, dec_seq, D_MODEL),
        ckv_p.reshape(depth, nb, n_pg, PAGE, KV_LORA),
        kr_p.reshape(depth, nb, n_pg, PAGE, QK_ROPE),
        mk.reshape(depth, nb, N_MEM, C_HEADS, C_HD),
        mv.reshape(depth, nb, N_MEM, C_HEADS, C_HD),
        ckv_s.reshape(depth, n_seq, dec_seq, KV_LORA),
        kr_s.reshape(depth, n_seq, dec_seq, QK_ROPE),
        av_s.reshape(depth, n_seq, dec_seq, A_HEADS, A_HD),
    )
```

```python
import functools

import jax
import jax.numpy as jnp
from jax import lax
from jax.experimental import pallas as pl
from jax.experimental.pallas import tpu as pltpu

F32 = jnp.float32
BF16 = jnp.bfloat16

LANES = 128
SUBLANES = 8
VMEM_BYTES_V7X = 64 * 1024 * 1024

D_MODEL = 1024
PAGE = 128
A_HEADS, A_HD = 4, 64
A_W = A_HEADS * A_HD
CHUNK = 128
B_HEADS = 4
V_HEAD = 128
QK_NOPE = 128
QK_ROPE = 64
HALF = QK_ROPE // 2
B_W = B_HEADS * V_HEAD
Q_LORA = 512
KV_LORA = 256
ROPE_BASE = 10000.0
SM_SCALE = (QK_NOPE + QK_ROPE) ** -0.5
N_MEM = 256
C_HEADS, C_HD = 4, 64
C_W = C_HEADS * C_HD
MEM_SCALE = C_HD ** -0.5
EPS = 1e-6
NEG_INF = float("-inf")

OFF_AU = 0
OFF_AV = OFF_AU + A_W
OFF_AG = OFF_AV + A_W
OFF_CQ = OFF_AG + A_W
OFF_CKV = OFF_CQ + Q_LORA
OFF_KR = OFF_CKV + KV_LORA
OFF_BG = OFF_KR + 2 * QK_ROPE
OFF_CQC = OFF_BG + B_W
OFF_CG = OFF_CQC + C_W
IN_EXT = OFF_CG + C_W
Q_HEAD_W = QK_NOPE + 2 * QK_ROPE
Q_EXT = B_HEADS * Q_HEAD_W

TQ = 512
DEC_ROWS = 16
DEC_RP = 64
DEC_SP = 16
DEC_NBUF = 2
TAIL_SEQ = 8


def _dot(a, b):
    return jnp.dot(a, b, preferred_element_type=F32)


def _dot_nt(a, b):
    return lax.dot_general(a, b, (((1,), (1,)), ((), ())), preferred_element_type=F32)


def _rmsnorm(x, g):
    return (x * lax.rsqrt(jnp.mean(x * x, axis=-1, keepdims=True) + EPS)) * g


def _silu(x):
    return x * (1.0 / (1.0 + jnp.exp(-x)))


def _tile_lanes(x, n):
    return jnp.concatenate([x] * n, axis=-1)


def _lane_head(shape, width):
    assert width & (width - 1) == 0
    return lax.shift_right_logical(lax.broadcasted_iota(jnp.int32, shape, len(shape) - 1), width.bit_length() - 1)


def _group_rmsnorm(v, ones_blk, g):
    sq = v * v
    hi = sq.astype(BF16)
    lo = (sq - hi.astype(F32)).astype(BF16)
    ssum = _dot(hi, ones_blk) + _dot(lo, ones_blk)
    return (v * lax.rsqrt(ssum * (1.0 / A_HD) + EPS)) * g


def _rope128(v, cs):
    w = v * cs
    return w + pltpu.roll(w, QK_ROPE, 1)


def _project(x, cs, g_pre, w_in, ones_blk, g_av, g_q, w_uq, g_kv):
    h = _rmsnorm(x, g_pre).astype(BF16)
    z = _dot(h, w_in)
    a_v = _group_rmsnorm(z[:, OFF_AV:OFF_AV + A_W], ones_blk, g_av)
    cq = _rmsnorm(z[:, OFF_CQ:OFF_CQ + Q_LORA], g_q).astype(BF16)
    q = _dot(cq, w_uq)
    ckv = _rmsnorm(z[:, OFF_CKV:OFF_CKV + KV_LORA], g_kv)
    kr = _rope128(z[:, OFF_KR:OFF_KR + 2 * QK_ROPE], cs)
    return z, a_v, q, ckv, kr


def _softmax_lanes(s):
    m = jnp.max(s, axis=-1, keepdims=True)
    p = jnp.exp(s - m)
    return p / jnp.sum(p, axis=-1, keepdims=True)


def _mem_attn_heads(cq, mk, mv):
    head = _lane_head(cq.shape, C_HD)
    out = jnp.zeros(cq.shape, F32)
    for hd in range(C_HEADS):
        qm = jnp.where(head == hd, cq, 0.0).astype(BF16)
        p = _softmax_lanes(_dot_nt(qm, mk) * MEM_SCALE)
        o = _dot(p.astype(BF16), mv)
        out = out + jnp.where(head == hd, o, 0.0)
    return out


def _finish(x, mix, w_out, g_post):
    return x + _rmsnorm(_dot(mix, w_out), g_post)


def _memkv_kernel(mem_ref, g_ref, w_ref, mk_ref, mv_ref):
    mn = _rmsnorm(mem_ref[...], g_ref[...]).astype(BF16)
    kv = _dot(mn, w_ref[...])
    mk_ref[...] = kv[:, :C_W]
    mv_ref[...] = kv[:, C_W:]


def _memkv_call(mem, g_mem, w_memkv):
    nb = mem.shape[0]
    return pl.pallas_call(
        _memkv_kernel,
        grid=(nb,),
        in_specs=[
            pl.BlockSpec((None, N_MEM, D_MODEL), lambda b: (b, 0, 0)),
            pl.BlockSpec((1, D_MODEL), lambda b: (0, 0)),
            pl.BlockSpec((D_MODEL, 2 * C_W), lambda b: (0, 0)),
        ],
        out_specs=[
            pl.BlockSpec((None, N_MEM, C_W), lambda b: (b, 0, 0)),
            pl.BlockSpec((None, N_MEM, C_W), lambda b: (b, 0, 0)),
        ],
        out_shape=[jax.ShapeDtypeStruct((nb, N_MEM, C_W), F32)] * 2,
        compiler_params=pltpu.CompilerParams(dimension_semantics=("arbitrary",)),
        name="memkv",
    )(mem, g_mem, w_memkv)


def _prompt_kernel(x_ref, cs_ref, mk_ref, mv_ref, g_pre_ref, w_in_ref, ones_ref, g_av_ref, w_s_ref,
                   b_s_ref, g_q_ref, w_uq_ref, g_kv_ref, w_ukv_ref, w_out_ref, g_post_ref,
                   y_ref, ckv_ref, kr_ref, k_scr, v_scr, mix_scr, q_scr, *state_scr):
    m_scr, acc_scr = state_scr[:B_HEADS], state_scr[B_HEADS:]
    i = pl.program_id(1)
    x = x_ref[...]
    cs = cs_ref[...]
    z, a_v, q, ckv, kr = _project(x, cs, g_pre_ref[...], w_in_ref[...], ones_ref[...], g_av_ref[...],
                                  g_q_ref[...], w_uq_ref[...], g_kv_ref[...])
    ckv_ref[...] = ckv
    kr_ref[...] = kr[:, :QK_ROPE]

    tril = (lax.broadcasted_iota(jnp.int32, (CHUNK, CHUNK), 1)
            <= lax.broadcasted_iota(jnp.int32, (CHUNK, CHUNK), 0))
    w_stack = jnp.concatenate(
        [jnp.where(tril, w_s_ref[hd], 0.0).astype(BF16) for hd in range(A_HEADS)], axis=0)
    head_a = _lane_head((CHUNK, A_W), A_HD)
    for c in range(TQ // CHUNK):
        rows = slice(c * CHUNK, (c + 1) * CHUNK)
        r = _dot(w_stack, a_v[rows].astype(BF16))
        s = b_s_ref[...]
        for hd in range(A_HEADS):
            s = s + jnp.where(head_a == hd, r[hd * CHUNK:(hd + 1) * CHUNK], 0.0)
        a_out = (z[rows, OFF_AU:OFF_AU + A_W] * s) * _silu(z[rows, OFF_AG:OFF_AG + A_W])
        mix_scr[rows, 0:A_W] = a_out.astype(BF16)

    kv = _dot(ckv.astype(BF16), w_ukv_ref[...])
    lane = lax.broadcasted_iota(jnp.int32, (TQ, 2 * QK_ROPE), 1)
    kr_pad = jnp.where(lane < QK_ROPE, kr, 0.0).astype(BF16)
    row0 = pl.multiple_of(i * TQ, TQ)
    for hd in range(B_HEADS):
        k_scr[hd, pl.ds(row0, TQ), 0:QK_NOPE] = kv[:, hd * QK_NOPE:(hd + 1) * QK_NOPE].astype(BF16)
        k_scr[hd, pl.ds(row0, TQ), QK_NOPE:] = kr_pad
        v_scr[hd, pl.ds(row0, TQ), 0:V_HEAD] = kv[:, B_W + hd * V_HEAD:B_W + (hd + 1) * V_HEAD].astype(BF16)
        v_scr[hd, pl.ds(row0, TQ), V_HEAD:] = jnp.ones((TQ, V_HEAD), BF16)

    for hd in range(B_HEADS):
        qh = q[:, hd * Q_HEAD_W:(hd + 1) * Q_HEAD_W]
        q_scr[hd, :, 0:QK_NOPE] = (qh[:, :QK_NOPE] * SM_SCALE).astype(BF16)
        q_scr[hd, :, QK_NOPE:] = (_rope128(qh[:, QK_NOPE:], cs) * SM_SCALE).astype(BF16)
    for hd in range(B_HEADS):
        m_scr[hd][...] = jnp.full((TQ, LANES), NEG_INF, F32)
        acc_scr[hd][...] = jnp.zeros((TQ, 2 * V_HEAD), F32)

    def attend(r0, mask):
        for hd in range(B_HEADS):
            s = _dot_nt(q_scr[hd], k_scr[hd, pl.ds(r0, TQ), :])
            if mask is not None:
                s = jnp.where(mask, s, NEG_INF)
            m = m_scr[hd][...]
            m_new = jnp.maximum(m, jnp.broadcast_to(jnp.max(s, axis=-1, keepdims=True), m.shape))
            p = jnp.exp(s - _tile_lanes(m_new, TQ // LANES)).astype(BF16)
            alpha = _tile_lanes(jnp.exp(m - m_new), 2 * V_HEAD // LANES)
            acc_scr[hd][...] = alpha * acc_scr[hd][...] + _dot(p, v_scr[hd, pl.ds(r0, TQ), :])
            m_scr[hd][...] = m_new

    def body(j, carry):
        attend(pl.multiple_of(j * TQ, TQ), None)
        return carry

    lax.fori_loop(0, i, body, 0)
    diag = (lax.broadcasted_iota(jnp.int32, (TQ, TQ), 1) <= lax.broadcasted_iota(jnp.int32, (TQ, TQ), 0))
    attend(row0, diag)
    for hd in range(B_HEADS):
        b_g = z[:, OFF_BG + hd * V_HEAD:OFF_BG + (hd + 1) * V_HEAD]
        acc = acc_scr[hd][...]
        mix_scr[:, A_W + hd * V_HEAD:A_W + (hd + 1) * V_HEAD] = (
            (acc[:, :V_HEAD] / acc[:, V_HEAD:]) * _silu(b_g)).astype(BF16)

    c_out = _mem_attn_heads(z[:, OFF_CQC:OFF_CQC + C_W], mk_ref[...].astype(BF16), mv_ref[...].astype(BF16))
    mix_scr[:, A_W + B_W:] = (c_out * _silu(z[:, OFF_CG:OFF_CG + C_W])).astype(BF16)

    y_ref[...] = _finish(x, mix_scr[...], w_out_ref[...], g_post_ref[...])


def _full(shape):
    return pl.BlockSpec(shape, lambda *_: (0,) * len(shape))


def _prompt_call(x, cs, mk, mv, g_pre, w_in, ones_blk, g_av, w_s, b_s, g_q, w_uq, g_kv, w_ukv, w_out, g_post):
    nb, seq, _ = x.shape
    nq = seq // TQ
    weights = (g_pre, w_in, ones_blk, g_av, w_s, b_s, g_q, w_uq, g_kv, w_ukv, w_out, g_post)
    return pl.pallas_call(
        _prompt_kernel,
        grid=(nb, nq),
        in_specs=[
            pl.BlockSpec((None, TQ, D_MODEL), lambda b, i: (b, i, 0)),
            pl.BlockSpec((TQ, 2 * QK_ROPE), lambda b, i: (i, 0)),
            pl.BlockSpec((None, N_MEM, C_W), lambda b, i: (b, 0, 0)),
            pl.BlockSpec((None, N_MEM, C_W), lambda b, i: (b, 0, 0)),
        ] + [_full(w.shape) for w in weights],
        out_specs=[
            pl.BlockSpec((None, TQ, D_MODEL), lambda b, i: (b, i, 0)),
            pl.BlockSpec((None, TQ, KV_LORA), lambda b, i: (b, i, 0)),
            pl.BlockSpec((None, TQ, QK_ROPE), lambda b, i: (b, i, 0)),
        ],
        out_shape=[
            jax.ShapeDtypeStruct((nb, seq, D_MODEL), F32),
            jax.ShapeDtypeStruct((nb, seq, KV_LORA), F32),
            jax.ShapeDtypeStruct((nb, seq, QK_ROPE), F32),
        ],
        scratch_shapes=[
            pltpu.VMEM((B_HEADS, seq, QK_NOPE + 2 * QK_ROPE), BF16),
            pltpu.VMEM((B_HEADS, seq, 2 * V_HEAD), BF16),
            pltpu.VMEM((TQ, D_MODEL), BF16),
            pltpu.VMEM((B_HEADS, TQ, QK_NOPE + 2 * QK_ROPE), BF16),
        ] + [pltpu.VMEM((TQ, LANES), F32)] * B_HEADS + [pltpu.VMEM((TQ, 2 * V_HEAD), F32)] * B_HEADS,
        compiler_params=pltpu.CompilerParams(
            dimension_semantics=("arbitrary", "arbitrary"),
            vmem_limit_bytes=VMEM_BYTES_V7X * 3 // 4),
        name="prompt_layer",
    )(x, cs, mk, mv, *weights)


def _sample_proj_kernel(x_ref, cs_ref, g_pre_ref, w_in_ref, ones_ref, g_av_ref, w00_ref, b0_ref,
                        g_q_ref, w_uq_ref, g_kv_ref, w_ukt_ref,
                        z_ref, av_ref, aout_ref, qlat_ref, qrope_ref, ckv_ref, kr_ref):
    cs = cs_ref[...]
    z, a_v, q, ckv, kr = _project(x_ref[...], cs, g_pre_ref[...], w_in_ref[...], ones_ref[...],
                                  g_av_ref[...], g_q_ref[...], w_uq_ref[...], g_kv_ref[...])
    z_ref[...] = z
    av_ref[...] = a_v
    ckv_ref[...] = ckv
    kr_ref[...] = kr[:, :QK_ROPE]
    s = w00_ref[...] * a_v + b0_ref[...]
    aout_ref[...] = (z[:, OFF_AU:OFF_AU + A_W] * s) * _silu(z[:, OFF_AG:OFF_AG + A_W])
    for hd in range(B_HEADS):
        qh = q[:, hd * Q_HEAD_W:(hd + 1) * Q_HEAD_W]
        qlat_ref[:, hd * KV_LORA:(hd + 1) * KV_LORA] = _dot(qh[:, :QK_NOPE].astype(BF16), w_ukt_ref[hd])
        qrope_ref[:, hd * 2 * QK_ROPE:(hd + 1) * 2 * QK_ROPE] = _rope128(qh[:, QK_NOPE:], cs)


def _sample_proj_call(x, cs, g_pre, w_in, ones_blk, g_av, w00, b0, g_q, w_uq, g_kv, w_ukt):
    n = x.shape[0]
    args = (x, cs, g_pre, w_in, ones_blk, g_av, w00, b0, g_q, w_uq, g_kv, w_ukt)
    widths = (IN_EXT, A_W, A_W, B_HEADS * KV_LORA, B_HEADS * 2 * QK_ROPE, KV_LORA, QK_ROPE)
    return pl.pallas_call(
        _sample_proj_kernel,
        grid=(1,),
        in_specs=[_full(a.shape) for a in args],
        out_specs=[_full((n, w)) for w in widths],
        out_shape=[jax.ShapeDtypeStruct((n, w), F32) for w in widths],
        compiler_params=pltpu.CompilerParams(dimension_semantics=("arbitrary",)),
        name="sample_proj",
    )(*args)


def _decode_kernel(pt_ref, qlat_ref, qrope_ref, cnew_ref, knew_ref, ckv_hbm, krt_hbm, o_ref,
                   cbuf, kbuf, sems, *, n_seq, n_regions):
    b = pl.program_id(0)

    def page_copies(seq, region, slot):
        out = []
        for k in range(DEC_RP):
            page = pt_ref[seq, region * DEC_RP + k]
            out.append(pltpu.make_async_copy(ckv_hbm.at[page], cbuf.at[slot, k], sems.at[0, slot]))
            out.append(pltpu.make_async_copy(krt_hbm.at[page], kbuf.at[slot, k], sems.at[1, slot]))
        return out

    def start(seq, region, slot):
        for cp in page_copies(seq, region, slot):
            cp.start()

    @pl.when(b == 0)
    def _():
        start(0, 0, 0)

    q_lat = qlat_ref[...]
    q_rope = qrope_ref[...]
    c_new = cnew_ref[...]
    k_new = knew_ref[...]
    s_new = (jnp.sum(q_lat * c_new, axis=-1, keepdims=True)
             + jnp.sum(q_rope * k_new, axis=-1, keepdims=True)) * SM_SCALE
    q_lat = q_lat.astype(BF16)
    q_rope = q_rope.astype(BF16)

    parts = []
    for r in range(n_regions):
        if r + 1 < n_regions:
            start(b, r + 1, (r + 1) % DEC_NBUF)
        else:
            @pl.when(b + 1 < n_seq)
            def _():
                start(b + 1, 0, 0)
        slot = r % DEC_NBUF
        for cp in page_copies(b, r, slot):
            cp.wait()
        for j in range(DEC_RP // DEC_SP):
            pages = slice(j * DEC_SP, (j + 1) * DEC_SP)
            ckv_j = cbuf[slot, pages].reshape(DEC_SP * PAGE, KV_LORA).astype(BF16)
            s_rope = jnp.concatenate(
                [_dot(q_rope, kbuf[slot, j * DEC_SP + k].astype(BF16)) for k in range(DEC_SP)], axis=1)
            s = (_dot_nt(q_lat, ckv_j) + s_rope) * SM_SCALE
            m_j = jnp.max(s, axis=-1, keepdims=True)
            p = jnp.exp(s - m_j)
            parts.append((m_j, jnp.sum(p, axis=-1, keepdims=True), _dot(p.astype(BF16), ckv_j)))

    m = s_new
    for m_j, _, _ in parts:
        m = jnp.maximum(m, m_j)
    w_new = jnp.exp(s_new - m)
    l = w_new
    acc = w_new * c_new
    for m_j, l_j, acc_j in parts:
        w_j = jnp.exp(m_j - m)
        l = l + w_j * l_j
        acc = acc + w_j * acc_j
    o_ref[...] = acc / l


def _decode_call(page_table, q_lat, q_rope, c_new, k_new, pool_ckv, pool_krt):
    n_seq, n_pages = page_table.shape
    assert n_pages % DEC_RP == 0 and DEC_RP % DEC_SP == 0
    n_regions = n_pages // DEC_RP
    assert n_regions % DEC_NBUF == 0
    grid_spec = pltpu.PrefetchScalarGridSpec(
        num_scalar_prefetch=1,
        grid=(n_seq,),
        in_specs=[
            pl.BlockSpec((None, DEC_ROWS, KV_LORA), lambda b, pt: (b, 0, 0)),
            pl.BlockSpec((None, DEC_ROWS, QK_ROPE), lambda b, pt: (b, 0, 0)),
            pl.BlockSpec((None, 1, KV_LORA), lambda b, pt: (b, 0, 0)),
            pl.BlockSpec((None, 1, QK_ROPE), lambda b, pt: (b, 0, 0)),
            pl.BlockSpec(memory_space=pl.ANY),
            pl.BlockSpec(memory_space=pl.ANY),
        ],
        out_specs=pl.BlockSpec((None, DEC_ROWS, KV_LORA), lambda b, pt: (b, 0, 0)),
        scratch_shapes=[
            pltpu.VMEM((DEC_NBUF, DEC_RP, PAGE, KV_LORA), F32),
            pltpu.VMEM((DEC_NBUF, DEC_RP, QK_ROPE, PAGE), F32),
            pltpu.SemaphoreType.DMA((2, DEC_NBUF)),
        ],
    )
    return pl.pallas_call(
        functools.partial(_decode_kernel, n_seq=n_seq, n_regions=n_regions),
        grid_spec=grid_spec,
        out_shape=jax.ShapeDtypeStruct((n_seq, DEC_ROWS, KV_LORA), F32),
        compiler_params=pltpu.CompilerParams(
            dimension_semantics=("arbitrary",),
            vmem_limit_bytes=VMEM_BYTES_V7X * 3 // 4),
        name="paged_decode",
    )(page_table, q_lat, q_rope, c_new, k_new, pool_ckv, pool_krt)


def _sample_tail_kernel(x_ref, z_ref, aout_ref, olat_ref, mkt_ref, mvt_ref, w_uv_ref, w_out_ref, g_post_ref,
                        y_ref, mix_scr, c_scr):
    z = z_ref[...]
    mix_scr[:, 0:A_W] = aout_ref[...].astype(BF16)
    for hd in range(B_HEADS):
        o = _dot(olat_ref[hd].astype(BF16), w_uv_ref[hd])
        b_g = z[:, OFF_BG + hd * V_HEAD:OFF_BG + (hd + 1) * V_HEAD]
        mix_scr[:, A_W + hd * V_HEAD:A_W + (hd + 1) * V_HEAD] = (o * _silu(b_g)).astype(BF16)
    row = lax.broadcasted_iota(jnp.int32, (SUBLANES, C_W), 0)
    own = _lane_head((SUBLANES, C_W), C_HD) == row
    for n in range(TAIL_SEQ):
        cq = z[n:n + 1, OFF_CQC:OFF_CQC + C_W]
        qm = jnp.where(own, jnp.broadcast_to(cq, (SUBLANES, C_W)), 0.0).astype(BF16)
        p = _softmax_lanes(_dot(qm, mkt_ref[n].astype(BF16)) * MEM_SCALE)
        o = _dot_nt(p.astype(BF16), mvt_ref[n].astype(BF16))
        c_scr[n:n + 1, :] = jnp.sum(jnp.where(own, o, 0.0), axis=0, keepdims=True)
    mix_scr[:, A_W + B_W:] = (c_scr[...] * _silu(z[:, OFF_CG:OFF_CG + C_W])).astype(BF16)
    y_ref[...] = _finish(x_ref[...], mix_scr[...], w_out_ref[...], g_post_ref[...])


def _sample_tail_call(x, z, a_out, o_lat, mem_kt, mem_vt, w_uv, w_out, g_post):
    n = x.shape[0]
    assert n % TAIL_SEQ == 0
    return pl.pallas_call(
        _sample_tail_kernel,
        grid=(n // TAIL_SEQ,),
        in_specs=[
            pl.BlockSpec((TAIL_SEQ, D_MODEL), lambda g: (g, 0)),
            pl.BlockSpec((TAIL_SEQ, IN_EXT), lambda g: (g, 0)),
            pl.BlockSpec((TAIL_SEQ, A_W), lambda g: (g, 0)),
            pl.BlockSpec((B_HEADS, TAIL_SEQ, KV_LORA), lambda g: (0, g, 0)),
            pl.BlockSpec((TAIL_SEQ, C_W, N_MEM), lambda g: (g, 0, 0)),
            pl.BlockSpec((TAIL_SEQ, C_W, N_MEM), lambda g: (g, 0, 0)),
            _full(w_uv.shape), _full(w_out.shape), _full(g_post.shape),
        ],
        out_specs=pl.BlockSpec((TAIL_SEQ, D_MODEL), lambda g: (g, 0)),
        out_shape=jax.ShapeDtypeStruct((n, D_MODEL), F32),
        scratch_shapes=[pltpu.VMEM((TAIL_SEQ, D_MODEL), BF16), pltpu.VMEM((TAIL_SEQ, C_W), F32)],
        compiler_params=pltpu.CompilerParams(dimension_semantics=("arbitrary",)),
        name="sample_tail",
    )(x, z, a_out, o_lat, mem_kt, mem_vt, w_uv, w_out, g_post)


def _swap_halves(w):
    return jnp.concatenate([w[..., HALF:], w[..., :HALF]], axis=-1)


def _rope_table(pos):
    inv = ROPE_BASE ** (-jnp.arange(HALF, dtype=F32) / HALF)
    ang = pos[:, None] * inv[None, :]
    cos, sin = jnp.cos(ang), jnp.sin(ang)
    return jnp.concatenate([cos, cos, -sin, sin], axis=-1)


def kernel(x_prompt, x_sample, mem_prompt, cache_ckv, cache_krope, cache_mem_k, cache_mem_v, page_table,
           g_pre, w_in, g_av, w_s, b_s, g_q, w_uq, g_kv, w_uk, w_uv, g_mem, w_mem_k, w_mem_v, w_out, g_post):
    depth = g_pre.shape[0]
    assert depth == 1
    nb, seq, _ = x_prompt.shape
    n_seq, dec_seq, _ = x_sample.shape
    assert dec_seq == 1 and seq % TQ == 0
    past_len = page_table.shape[1] * PAGE
    l = 0

    row = lambda g: g.reshape(1, -1)
    kr_cols = w_in[l][:, OFF_KR:OFF_KR + QK_ROPE]
    w_in_ext = jnp.concatenate(
        [w_in[l][:, :OFF_KR + QK_ROPE], _swap_halves(kr_cols), w_in[l][:, OFF_KR + QK_ROPE:]], axis=1).astype(BF16)
    w_uq_l = w_uq[l]
    w_uq_ext = jnp.concatenate(
        [w_uq_l, _swap_halves(w_uq_l[..., QK_NOPE:])], axis=-1).reshape(Q_LORA, Q_EXT).astype(BF16)
    w_ukv = jnp.concatenate([w_uk[l].reshape(KV_LORA, B_W), w_uv[l].reshape(KV_LORA, B_W)], axis=1).astype(BF16)
    w_ukt = jnp.transpose(w_uk[l], (1, 2, 0)).astype(BF16)
    w_uv_h = jnp.transpose(w_uv[l], (1, 0, 2)).astype(BF16)
    w_out_b = w_out[l].astype(BF16)
    w_memkv = jnp.concatenate(
        [w_mem_k[l].reshape(D_MODEL, C_W), w_mem_v[l].reshape(D_MODEL, C_W)], axis=1).astype(BF16)
    ones_blk = (jnp.arange(A_W)[:, None] // A_HD == jnp.arange(A_W)[None, :] // A_HD).astype(BF16)
    g_av_l = g_av[l].reshape(1, A_W)
    b_s_lanes = jnp.repeat(jnp.transpose(b_s[l]), A_HD, axis=1)
    w00 = jnp.repeat(w_s[l][:, 0, 0], A_HD).reshape(1, A_W)
    b0 = b_s_lanes[0:1]

    cs_p = _rope_table(jnp.arange(seq, dtype=F32))
    cs_s = jnp.broadcast_to(_rope_table(past_len + jnp.arange(dec_seq, dtype=F32)), (n_seq, 2 * QK_ROPE))

    mk, mv = _memkv_call(mem_prompt, row(g_mem[l]), w_memkv)
    y_p, ckv_p, kr_p = _prompt_call(
        x_prompt, cs_p, mk, mv, row(g_pre[l]), w_in_ext, ones_blk, g_av_l, w_s[l], b_s_lanes,
        row(g_q[l]), w_uq_ext, row(g_kv[l]), w_ukv, w_out_b, row(g_post[l]))

    x_s = x_sample.reshape(n_seq, D_MODEL)
    z_s, av_s, aout_s, qlat_s, qrope_s, ckv_s, kr_s = _sample_proj_call(
        x_s, cs_s, row(g_pre[l]), w_in_ext, ones_blk, g_av_l, w00, b0, row(g_q[l]), w_uq_ext, row(g_kv[l]), w_ukt)
    pad_rows = ((0, 0), (0, DEC_ROWS - B_HEADS), (0, 0))
    q_lat = jnp.pad(qlat_s.reshape(n_seq, B_HEADS, KV_LORA), pad_rows)
    q_rope = jnp.pad(qrope_s.reshape(n_seq, B_HEADS, 2 * QK_ROPE)[..., :QK_ROPE], pad_rows)
    pool_krt = jnp.swapaxes(cache_krope[l], 1, 2)
    mem_kt = jnp.transpose(cache_mem_k[l], (0, 2, 3, 1)).reshape(n_seq, C_W, N_MEM)
    mem_vt = jnp.transpose(cache_mem_v[l], (0, 2, 3, 1)).reshape(n_seq, C_W, N_MEM)
    o_lat = _decode_call(page_table, q_lat, q_rope, ckv_s.reshape(n_seq, 1, KV_LORA),
                         kr_s.reshape(n_seq, 1, QK_ROPE), cache_ckv[l], pool_krt)
    o_lat_h = jnp.transpose(o_lat[:, :B_HEADS], (1, 0, 2))
    y_s = _sample_tail_call(x_s, z_s, aout_s, o_lat_h, mem_kt, mem_vt, w_uv_h, w_out_b, row(g_post[l]))

    n_pg = seq // PAGE
    return (
        y_p,
        y_s.reshape(n_seq, dec_seq, D_MODEL),
        ckv_p.reshape(depth, nb, n_pg, PAGE, KV_LORA),
        kr_p.reshape(depth, nb, n_pg, PAGE, QK_ROPE),
        mk.reshape(depth, nb, N_MEM, C_HEADS, C_HD),
        mv.reshape(depth, nb, N_MEM, C_HEADS, C_HD),
        ckv_s.reshape(depth, n_seq, dec_seq, KV_LORA),
        kr_s.reshape(depth, n_seq, dec_seq, QK_ROPE),
        av_s.reshape(depth, n_seq, dec_seq, A_HEADS, A_HD),
    )
```

```python
import functools

import jax
import jax.numpy as jnp
from jax import lax
from jax.experimental import pallas as pl
from jax.experimental.pallas import tpu as pltpu

F32 = jnp.float32
BF16 = jnp.bfloat16

LANES = 128
SUBLANES = 8
VMEM_BYTES_V7X = 64 * 1024 * 1024

D_MODEL = 1024
PAGE = 128
A_HEADS, A_HD = 4, 64
A_W = A_HEADS * A_HD
CHUNK = 128
B_HEADS = 4
V_HEAD = 128
QK_NOPE = 128
QK_ROPE = 64
HALF = QK_ROPE // 2
B_W = B_HEADS * V_HEAD
Q_LORA = 512
KV_LORA = 256
ROPE_BASE = 10000.0
SM_SCALE = (QK_NOPE + QK_ROPE) ** -0.5
N_MEM = 256
C_HEADS, C_HD = 4, 64
C_W = C_HEADS * C_HD
MEM_SCALE = C_HD ** -0.5
EPS = 1e-6
NEG_INF = float("-inf")

OFF_AU = 0
OFF_AV = OFF_AU + A_W
OFF_AG = OFF_AV + A_W
OFF_CQ = OFF_AG + A_W
OFF_CKV = OFF_CQ + Q_LORA
OFF_KR = OFF_CKV + KV_LORA
OFF_BG = OFF_KR + 2 * QK_ROPE
OFF_CQC = OFF_BG + B_W
OFF_CG = OFF_CQC + C_W
IN_EXT = OFF_CG + C_W
Q_HEAD_W = QK_NOPE + 2 * QK_ROPE
Q_EXT = B_HEADS * Q_HEAD_W

TQ = 512
DEC_ROWS = SUBLANES
DEC_RP = 64
DEC_SP = 32
DEC_NBUF = 3
DEC_AHEAD = DEC_NBUF - 1
TAIL_SEQ = 8


def _dot(a, b):
    return jnp.dot(a, b, preferred_element_type=F32)


def _dot_nt(a, b):
    return lax.dot_general(a, b, (((1,), (1,)), ((), ())), preferred_element_type=F32)


def _rmsnorm(x, g):
    return (x * lax.rsqrt(jnp.mean(x * x, axis=-1, keepdims=True) + EPS)) * g


def _silu(x):
    return x * (1.0 / (1.0 + jnp.exp(-x)))


def _tile_lanes(x, n):
    return jnp.concatenate([x] * n, axis=-1)


def _lane_head(shape, width):
    assert width & (width - 1) == 0
    return lax.shift_right_logical(lax.broadcasted_iota(jnp.int32, shape, len(shape) - 1), width.bit_length() - 1)


def _group_rmsnorm(v, ones_blk, g):
    sq = v * v
    hi = sq.astype(BF16)
    lo = (sq - hi.astype(F32)).astype(BF16)
    ssum = _dot(hi, ones_blk) + _dot(lo, ones_blk)
    return (v * lax.rsqrt(ssum * (1.0 / A_HD) + EPS)) * g


def _rope128(v, cs):
    w = v * cs
    return w + pltpu.roll(w, QK_ROPE, 1)


def _project(x, cs, g_pre, w_in, ones_blk, g_av, g_q, w_uq, g_kv):
    h = _rmsnorm(x, g_pre).astype(BF16)
    z = _dot(h, w_in)
    a_v = _group_rmsnorm(z[:, OFF_AV:OFF_AV + A_W], ones_blk, g_av)
    cq = _rmsnorm(z[:, OFF_CQ:OFF_CQ + Q_LORA], g_q).astype(BF16)
    q = _dot(cq, w_uq)
    ckv = _rmsnorm(z[:, OFF_CKV:OFF_CKV + KV_LORA], g_kv)
    kr = _rope128(z[:, OFF_KR:OFF_KR + 2 * QK_ROPE], cs)
    return z, a_v, q, ckv, kr


def _softmax_lanes(s):
    m = jnp.max(s, axis=-1, keepdims=True)
    p = jnp.exp(s - m)
    return p / jnp.sum(p, axis=-1, keepdims=True)


def _mem_attn_heads(cq, mk, mv):
    head = _lane_head(cq.shape, C_HD)
    out = jnp.zeros(cq.shape, F32)
    for hd in range(C_HEADS):
        qm = jnp.where(head == hd, cq, 0.0).astype(BF16)
        p = _softmax_lanes(_dot_nt(qm, mk) * MEM_SCALE)
        o = _dot(p.astype(BF16), mv)
        out = out + jnp.where(head == hd, o, 0.0)
    return out


def _finish(x, mix, w_out, g_post):
    return x + _rmsnorm(_dot(mix, w_out), g_post)


def _memkv_kernel(mem_ref, g_ref, w_ref, mk_ref, mv_ref):
    mn = _rmsnorm(mem_ref[...], g_ref[...]).astype(BF16)
    kv = _dot(mn, w_ref[...])
    mk_ref[...] = kv[:, :C_W]
    mv_ref[...] = kv[:, C_W:]


def _memkv_call(mem, g_mem, w_memkv):
    nb = mem.shape[0]
    return pl.pallas_call(
        _memkv_kernel,
        grid=(nb,),
        in_specs=[
            pl.BlockSpec((None, N_MEM, D_MODEL), lambda b: (b, 0, 0)),
            pl.BlockSpec((1, D_MODEL), lambda b: (0, 0)),
            pl.BlockSpec((D_MODEL, 2 * C_W), lambda b: (0, 0)),
        ],
        out_specs=[
            pl.BlockSpec((None, N_MEM, C_W), lambda b: (b, 0, 0)),
            pl.BlockSpec((None, N_MEM, C_W), lambda b: (b, 0, 0)),
        ],
        out_shape=[jax.ShapeDtypeStruct((nb, N_MEM, C_W), F32)] * 2,
        compiler_params=pltpu.CompilerParams(dimension_semantics=("arbitrary",)),
        name="memkv",
    )(mem, g_mem, w_memkv)


def _prompt_kernel(x_ref, cs_ref, mk_ref, mv_ref, g_pre_ref, w_in_ref, ones_ref, g_av_ref, w_s_ref,
                   b_s_ref, g_q_ref, w_uq_ref, g_kv_ref, w_ukv_ref, w_out_ref, g_post_ref,
                   y_ref, ckv_ref, kr_ref, k_scr, v_scr, mix_scr, q_scr, *state_scr):
    m_scr, acc_scr = state_scr[:B_HEADS], state_scr[B_HEADS:]
    i = pl.program_id(1)
    x = x_ref[...]
    cs = cs_ref[...]
    z, a_v, q, ckv, kr = _project(x, cs, g_pre_ref[...], w_in_ref[...], ones_ref[...], g_av_ref[...],
                                  g_q_ref[...], w_uq_ref[...], g_kv_ref[...])
    ckv_ref[...] = ckv
    for pg in range(TQ // PAGE):
        kr_ref[pg] = kr[pg * PAGE:(pg + 1) * PAGE, :].T[:QK_ROPE, :]

    tril = (lax.broadcasted_iota(jnp.int32, (CHUNK, CHUNK), 1)
            <= lax.broadcasted_iota(jnp.int32, (CHUNK, CHUNK), 0))
    w_stack = jnp.concatenate(
        [jnp.where(tril, w_s_ref[hd], 0.0).astype(BF16) for hd in range(A_HEADS)], axis=0)
    head_a = _lane_head((CHUNK, A_W), A_HD)
    for c in range(TQ // CHUNK):
        rows = slice(c * CHUNK, (c + 1) * CHUNK)
        r = _dot(w_stack, a_v[rows].astype(BF16))
        s = b_s_ref[...]
        for hd in range(A_HEADS):
            s = s + jnp.where(head_a == hd, r[hd * CHUNK:(hd + 1) * CHUNK], 0.0)
        a_out = (z[rows, OFF_AU:OFF_AU + A_W] * s) * _silu(z[rows, OFF_AG:OFF_AG + A_W])
        mix_scr[rows, 0:A_W] = a_out.astype(BF16)

    kv = _dot(ckv.astype(BF16), w_ukv_ref[...])
    lane = lax.broadcasted_iota(jnp.int32, (TQ, 2 * QK_ROPE), 1)
    kr_pad = jnp.where(lane < QK_ROPE, kr, 0.0).astype(BF16)
    row0 = pl.multiple_of(i * TQ, TQ)
    for hd in range(B_HEADS):
        k_scr[hd, pl.ds(row0, TQ), 0:QK_NOPE] = kv[:, hd * QK_NOPE:(hd + 1) * QK_NOPE].astype(BF16)
        k_scr[hd, pl.ds(row0, TQ), QK_NOPE:] = kr_pad
        v_scr[hd, pl.ds(row0, TQ), 0:V_HEAD] = kv[:, B_W + hd * V_HEAD:B_W + (hd + 1) * V_HEAD].astype(BF16)
        v_scr[hd, pl.ds(row0, TQ), V_HEAD:] = jnp.ones((TQ, V_HEAD), BF16)

    for hd in range(B_HEADS):
        qh = q[:, hd * Q_HEAD_W:(hd + 1) * Q_HEAD_W]
        q_scr[hd, :, 0:QK_NOPE] = (qh[:, :QK_NOPE] * SM_SCALE).astype(BF16)
        q_scr[hd, :, QK_NOPE:] = (_rope128(qh[:, QK_NOPE:], cs) * SM_SCALE).astype(BF16)
    for hd in range(B_HEADS):
        m_scr[hd][...] = jnp.full((TQ, LANES), NEG_INF, F32)
        acc_scr[hd][...] = jnp.zeros((TQ, 2 * V_HEAD), F32)

    def attend(rows, k0, n_keys, mask):
        for hd in range(B_HEADS):
            s = _dot_nt(q_scr[hd, rows], k_scr[hd, pl.ds(k0, n_keys), :])
            if mask is not None:
                s = jnp.where(mask, s, NEG_INF)
            m = m_scr[hd][rows]
            m_new = jnp.maximum(m, jnp.broadcast_to(jnp.max(s, axis=-1, keepdims=True), m.shape))
            p = jnp.exp(s - _tile_lanes(m_new, n_keys // LANES)).astype(BF16)
            alpha = _tile_lanes(jnp.exp(m - m_new), 2 * V_HEAD // LANES)
            acc_scr[hd][rows] = alpha * acc_scr[hd][rows] + _dot(p, v_scr[hd, pl.ds(k0, n_keys), :])
            m_scr[hd][rows] = m_new

    def body(j, carry):
        attend(slice(0, TQ), pl.multiple_of(j * TQ, TQ), TQ, None)
        return carry

    lax.fori_loop(0, i, body, 0)
    hq = TQ // 2
    def visible(n_keys, row_offset):
        col = lax.broadcasted_iota(jnp.int32, (hq, n_keys), 1)
        return col <= lax.broadcasted_iota(jnp.int32, (hq, n_keys), 0) + row_offset

    attend(slice(0, hq), row0, hq, visible(hq, 0))
    attend(slice(hq, TQ), row0, TQ, visible(TQ, hq))
    for hd in range(B_HEADS):
        b_g = z[:, OFF_BG + hd * V_HEAD:OFF_BG + (hd + 1) * V_HEAD]
        acc = acc_scr[hd][...]
        mix_scr[:, A_W + hd * V_HEAD:A_W + (hd + 1) * V_HEAD] = (
            (acc[:, :V_HEAD] / acc[:, V_HEAD:]) * _silu(b_g)).astype(BF16)

    c_out = _mem_attn_heads(z[:, OFF_CQC:OFF_CQC + C_W], mk_ref[...].astype(BF16), mv_ref[...].astype(BF16))
    mix_scr[:, A_W + B_W:] = (c_out * _silu(z[:, OFF_CG:OFF_CG + C_W])).astype(BF16)

    y_ref[...] = _finish(x, mix_scr[...], w_out_ref[...], g_post_ref[...])


def _full(shape):
    return pl.BlockSpec(shape, lambda *_: (0,) * len(shape))


def _prompt_call(x, cs, mk, mv, g_pre, w_in, ones_blk, g_av, w_s, b_s, g_q, w_uq, g_kv, w_ukv, w_out, g_post):
    nb, seq, _ = x.shape
    nq = seq // TQ
    weights = (g_pre, w_in, ones_blk, g_av, w_s, b_s, g_q, w_uq, g_kv, w_ukv, w_out, g_post)
    return pl.pallas_call(
        _prompt_kernel,
        grid=(nb, nq),
        in_specs=[
            pl.BlockSpec((None, TQ, D_MODEL), lambda b, i: (b, i, 0)),
            pl.BlockSpec((TQ, 2 * QK_ROPE), lambda b, i: (i, 0)),
            pl.BlockSpec((None, N_MEM, C_W), lambda b, i: (b, 0, 0)),
            pl.BlockSpec((None, N_MEM, C_W), lambda b, i: (b, 0, 0)),
        ] + [_full(w.shape) for w in weights],
        out_specs=[
            pl.BlockSpec((None, TQ, D_MODEL), lambda b, i: (b, i, 0)),
            pl.BlockSpec((None, TQ, KV_LORA), lambda b, i: (b, i, 0)),
            pl.BlockSpec((None, TQ // PAGE, QK_ROPE, PAGE), lambda b, i: (b, i, 0, 0)),
        ],
        out_shape=[
            jax.ShapeDtypeStruct((nb, seq, D_MODEL), F32),
            jax.ShapeDtypeStruct((nb, seq, KV_LORA), F32),
            jax.ShapeDtypeStruct((nb, seq // PAGE, QK_ROPE, PAGE), F32),
        ],
        scratch_shapes=[
            pltpu.VMEM((B_HEADS, seq, QK_NOPE + 2 * QK_ROPE), BF16),
            pltpu.VMEM((B_HEADS, seq, 2 * V_HEAD), BF16),
            pltpu.VMEM((TQ, D_MODEL), BF16),
            pltpu.VMEM((B_HEADS, TQ, QK_NOPE + 2 * QK_ROPE), BF16),
        ] + [pltpu.VMEM((TQ, LANES), F32)] * B_HEADS + [pltpu.VMEM((TQ, 2 * V_HEAD), F32)] * B_HEADS,
        compiler_params=pltpu.CompilerParams(
            dimension_semantics=("arbitrary", "arbitrary"),
            vmem_limit_bytes=VMEM_BYTES_V7X * 3 // 4),
        name="prompt_layer",
    )(x, cs, mk, mv, *weights)


def _sample_proj_kernel(x_ref, cs_ref, g_pre_ref, w_in_ref, ones_ref, g_av_ref, w00_ref, b0_ref,
                        g_q_ref, w_uq_ref, g_kv_ref, w_ukt_ref,
                        z_ref, av_ref, aout_ref, qlat_ref, qrope_ref, ckv_ref, kr_ref, krt_ref):
    cs = cs_ref[...]
    z, a_v, q, ckv, kr = _project(x_ref[...], cs, g_pre_ref[...], w_in_ref[...], ones_ref[...],
                                  g_av_ref[...], g_q_ref[...], w_uq_ref[...], g_kv_ref[...])
    z_ref[...] = z
    av_ref[...] = a_v
    ckv_ref[...] = ckv
    kr_ref[...] = kr[:, :QK_ROPE]
    krt_ref[...] = kr.T[:QK_ROPE, :]
    s = w00_ref[...] * a_v + b0_ref[...]
    aout_ref[...] = (z[:, OFF_AU:OFF_AU + A_W] * s) * _silu(z[:, OFF_AG:OFF_AG + A_W])
    qlat_ref[...] = jnp.zeros(qlat_ref.shape, F32)
    qrope_ref[...] = jnp.zeros(qrope_ref.shape, F32)
    for hd in range(B_HEADS):
        qh = q[:, hd * Q_HEAD_W:(hd + 1) * Q_HEAD_W]
        qlat_ref[:, hd, :] = _dot(qh[:, :QK_NOPE].astype(BF16), w_ukt_ref[hd])
        qrope_ref[:, hd, :] = _rope128(qh[:, QK_NOPE:], cs)[:, :QK_ROPE]


def _sample_proj_call(x, cs, g_pre, w_in, ones_blk, g_av, w00, b0, g_q, w_uq, g_kv, w_ukt):
    n = x.shape[0]
    args = (x, cs, g_pre, w_in, ones_blk, g_av, w00, b0, g_q, w_uq, g_kv, w_ukt)
    shapes = ((n, IN_EXT), (n, A_W), (n, A_W), (n, DEC_ROWS, KV_LORA), (n, DEC_ROWS, QK_ROPE),
              (n, KV_LORA), (n, QK_ROPE), (QK_ROPE, n))
    return pl.pallas_call(
        _sample_proj_kernel,
        grid=(1,),
        in_specs=[_full(a.shape) for a in args],
        out_specs=[_full(s) for s in shapes],
        out_shape=[jax.ShapeDtypeStruct(s, F32) for s in shapes],
        compiler_params=pltpu.CompilerParams(dimension_semantics=("arbitrary",)),
        name="sample_proj",
    )(*args)


def _decode_kernel(pt_ref, qlat_ref, qrope_ref, cnew_ref, knew_ref, z_ref, mkt_ref, mvt_ref, ckv_hbm, krt_hbm,
                   o_ref, c_ref, cbuf, kbuf, sems, *, n_seq, n_regions):
    b = pl.program_id(0)
    n_sub = DEC_RP // DEC_SP

    def page_copies(seq, region, slot, sub):
        out = []
        for k in range(sub * DEC_SP, (sub + 1) * DEC_SP):
            page = pt_ref[seq, region * DEC_RP + k]
            out.append((pltpu.make_async_copy(ckv_hbm.at[page], cbuf.at[slot, k], sems.at[0, slot]), 0))
            out.append((pltpu.make_async_copy(krt_hbm.at[page], kbuf.at[slot, k], sems.at[1, slot]), 1))
        return out

    def ahead_of(r):
        seq = b + (r + DEC_AHEAD) // n_regions
        seq = jnp.where(seq < n_seq, seq, seq - n_seq)
        return seq, (r + DEC_AHEAD) % n_regions, lax.rem(b * n_regions + r + DEC_AHEAD, DEC_NBUF)

    @pl.when(b == 0)
    def _():
        for g in range(DEC_AHEAD):
            for sub in range(n_sub):
                for cp, prio in page_copies(g // n_regions, g % n_regions, g % DEC_NBUF, sub):
                    cp.start(priority=prio)

    q_lat = qlat_ref[...]
    q_rope = qrope_ref[...]
    c_new = cnew_ref[...]
    k_new = knew_ref[...]
    s_new = (jnp.sum(q_lat * c_new, axis=-1, keepdims=True)
             + jnp.sum(q_rope * k_new, axis=-1, keepdims=True)) * SM_SCALE
    q_lat = q_lat.astype(BF16)
    q_rope = q_rope.astype(BF16)

    def mem_attention():
        row = lax.broadcasted_iota(jnp.int32, (SUBLANES, C_W), 0)
        own = _lane_head((SUBLANES, C_W), C_HD) == row
        cq = jnp.broadcast_to(z_ref[pl.ds(b, 1), OFF_CQC:OFF_CQC + C_W], (SUBLANES, C_W))
        p = _softmax_lanes(_dot(jnp.where(own, cq, 0.0).astype(BF16), mkt_ref[...].astype(BF16)) * MEM_SCALE)
        o = _dot_nt(p.astype(BF16), mvt_ref[...].astype(BF16))
        c_ref[...] = jnp.sum(jnp.where(own, o, 0.0), axis=0, keepdims=True)

    parts = []
    for r in range(n_regions):
        slot = lax.rem(b * n_regions + r, DEC_NBUF)
        for sub in range(n_sub):
            for cp, _ in page_copies(b, r, slot, sub):
                cp.wait()
        nxt_seq, nxt_region, nxt_slot = ahead_of(r)
        if r == 0:
            mem_attention()
        for j in range(n_sub):
            ckv_f = cbuf[slot, pl.ds(j * DEC_SP, DEC_SP)].reshape(DEC_SP * PAGE, KV_LORA)
            ckv_j = ckv_f.astype(BF16)
            s_rope = jnp.concatenate(
                [_dot(q_rope, kbuf[slot, j * DEC_SP + k].astype(BF16)) for k in range(DEC_SP)], axis=1)
            s = (_dot(q_lat, ckv_f.T.astype(BF16)) + s_rope) * SM_SCALE
            m_j = jnp.max(s, axis=-1, keepdims=True)
            p = jnp.exp(s - m_j)
            parts.append((m_j, jnp.sum(p, axis=-1, keepdims=True), _dot(p.astype(BF16), ckv_j)))
        for sub in range(n_sub):
            for cp, prio in page_copies(nxt_seq, nxt_region, nxt_slot, sub):
                cp.start(priority=prio)

    m = s_new
    for m_j, _, _ in parts:
        m = jnp.maximum(m, m_j)
    w_new = jnp.exp(s_new - m)
    l = w_new
    acc = w_new * c_new
    for m_j, l_j, acc_j in parts:
        w_j = jnp.exp(m_j - m)
        l = l + w_j * l_j
        acc = acc + w_j * acc_j
    o_ref[...] = acc / l

    @pl.when(b == n_seq - 1)
    def _():
        for r in range(n_regions):
            seq, region, slot = ahead_of(r)
            for sub in range(n_sub):
                for cp, _ in page_copies(seq, region, slot, sub):
                    cp.wait()


def _decode_call(page_table, q_lat, q_rope, c_new, k_new, z, mem_kt, mem_vt, pool_ckv, pool_krt):
    n_seq, n_pages = page_table.shape
    assert n_pages % DEC_RP == 0 and DEC_RP % DEC_SP == 0
    n_regions = n_pages // DEC_RP
    assert DEC_AHEAD <= n_regions * n_seq
    grid_spec = pltpu.PrefetchScalarGridSpec(
        num_scalar_prefetch=1,
        grid=(n_seq,),
        in_specs=[
            pl.BlockSpec((None, DEC_ROWS, KV_LORA), lambda b, pt: (b, 0, 0)),
            pl.BlockSpec((None, DEC_ROWS, QK_ROPE), lambda b, pt: (b, 0, 0)),
            pl.BlockSpec((None, 1, KV_LORA), lambda b, pt: (b, 0, 0)),
            pl.BlockSpec((None, 1, QK_ROPE), lambda b, pt: (b, 0, 0)),
            pl.BlockSpec(z.shape, lambda b, pt: (0, 0)),
            pl.BlockSpec((None, C_W, N_MEM), lambda b, pt: (b, 0, 0)),
            pl.BlockSpec((None, C_W, N_MEM), lambda b, pt: (b, 0, 0)),
            pl.BlockSpec(memory_space=pl.ANY),
            pl.BlockSpec(memory_space=pl.ANY),
        ],
        out_specs=[
            pl.BlockSpec((None, DEC_ROWS, KV_LORA), lambda b, pt: (b, 0, 0)),
            pl.BlockSpec((None, 1, C_W), lambda b, pt: (b, 0, 0)),
        ],
        scratch_shapes=[
            pltpu.VMEM((DEC_NBUF, DEC_RP, PAGE, KV_LORA), F32),
            pltpu.VMEM((DEC_NBUF, DEC_RP, QK_ROPE, PAGE), F32),
            pltpu.SemaphoreType.DMA((2, DEC_NBUF)),
        ],
    )
    return pl.pallas_call(
        functools.partial(_decode_kernel, n_seq=n_seq, n_regions=n_regions),
        grid_spec=grid_spec,
        out_shape=[
            jax.ShapeDtypeStruct((n_seq, DEC_ROWS, KV_LORA), F32),
            jax.ShapeDtypeStruct((n_seq, 1, C_W), F32),
        ],
        compiler_params=pltpu.CompilerParams(
            dimension_semantics=("arbitrary",),
            vmem_limit_bytes=VMEM_BYTES_V7X * 7 // 8),
        name="paged_decode",
    )(page_table, q_lat, q_rope, c_new, k_new, z, mem_kt, mem_vt, pool_ckv, pool_krt)


def _sample_tail_kernel(x_ref, z_ref, aout_ref, olat_ref, cout_ref, w_uv_ref, w_out_ref, g_post_ref, y_ref, mix_scr):
    z = z_ref[...]
    mix_scr[:, 0:A_W] = aout_ref[...].astype(BF16)
    for hd in range(B_HEADS):
        o = _dot(olat_ref[:, hd, :].astype(BF16), w_uv_ref[hd])
        b_g = z[:, OFF_BG + hd * V_HEAD:OFF_BG + (hd + 1) * V_HEAD]
        mix_scr[:, A_W + hd * V_HEAD:A_W + (hd + 1) * V_HEAD] = (o * _silu(b_g)).astype(BF16)
    mix_scr[:, A_W + B_W:] = (cout_ref[...] * _silu(z[:, OFF_CG:OFF_CG + C_W])).astype(BF16)
    y_ref[...] = _finish(x_ref[...], mix_scr[...], w_out_ref[...], g_post_ref[...])


def _sample_tail_call(x, z, a_out, o_lat, c_out, w_uv, w_out, g_post):
    n = x.shape[0]
    args = (x, z, a_out, o_lat, c_out, w_uv, w_out, g_post)
    return pl.pallas_call(
        _sample_tail_kernel,
        grid=(1,),
        in_specs=[_full(a.shape) for a in args],
        out_specs=_full((n, D_MODEL)),
        out_shape=jax.ShapeDtypeStruct((n, D_MODEL), F32),
        scratch_shapes=[pltpu.VMEM((n, D_MODEL), BF16)],
        compiler_params=pltpu.CompilerParams(dimension_semantics=("arbitrary",)),
        name="sample_tail",
    )(*args)


def _swap_halves(w):
    return jnp.concatenate([w[..., HALF:], w[..., :HALF]], axis=-1)


def _rope_table(pos):
    inv = ROPE_BASE ** (-jnp.arange(HALF, dtype=F32) / HALF)
    ang = pos[:, None] * inv[None, :]
    cos, sin = jnp.cos(ang), jnp.sin(ang)
    return jnp.concatenate([cos, cos, -sin, sin], axis=-1)


def kernel(x_prompt, x_sample, mem_prompt, cache_ckv, cache_krope, cache_mem_k, cache_mem_v, page_table,
           g_pre, w_in, g_av, w_s, b_s, g_q, w_uq, g_kv, w_uk, w_uv, g_mem, w_mem_k, w_mem_v, w_out, g_post):
    depth = g_pre.shape[0]
    assert depth == 1
    nb, seq, _ = x_prompt.shape
    n_seq, dec_seq, _ = x_sample.shape
    assert dec_seq == 1 and seq % TQ == 0
    past_len = page_table.shape[1] * PAGE
    l = 0

    row = lambda g: g.reshape(1, -1)
    kr_cols = w_in[l][:, OFF_KR:OFF_KR + QK_ROPE]
    w_in_ext = jnp.concatenate(
        [w_in[l][:, :OFF_KR + QK_ROPE], _swap_halves(kr_cols), w_in[l][:, OFF_KR + QK_ROPE:]], axis=1).astype(BF16)
    w_uq_l = w_uq[l]
    w_uq_ext = jnp.concatenate(
        [w_uq_l, _swap_halves(w_uq_l[..., QK_NOPE:])], axis=-1).reshape(Q_LORA, Q_EXT).astype(BF16)
    w_ukv = jnp.concatenate([w_uk[l].reshape(KV_LORA, B_W), w_uv[l].reshape(KV_LORA, B_W)], axis=1).astype(BF16)
    w_ukt = jnp.transpose(w_uk[l], (1, 2, 0)).astype(BF16)
    w_uv_h = jnp.transpose(w_uv[l], (1, 0, 2)).astype(BF16)
    w_out_b = w_out[l].astype(BF16)
    w_memkv = jnp.concatenate(
        [w_mem_k[l].reshape(D_MODEL, C_W), w_mem_v[l].reshape(D_MODEL, C_W)], axis=1).astype(BF16)
    ones_blk = (jnp.arange(A_W)[:, None] // A_HD == jnp.arange(A_W)[None, :] // A_HD).astype(BF16)
    g_av_l = g_av[l].reshape(1, A_W)
    b_s_lanes = jnp.repeat(jnp.transpose(b_s[l]), A_HD, axis=1)
    w00 = jnp.repeat(w_s[l][:, 0, 0], A_HD).reshape(1, A_W)
    b0 = b_s_lanes[0:1]

    cs_p = _rope_table(jnp.arange(seq, dtype=F32))
    cs_s = jnp.broadcast_to(_rope_table(past_len + jnp.arange(dec_seq, dtype=F32)), (n_seq, 2 * QK_ROPE))

    mk, mv = _memkv_call(mem_prompt, row(g_mem[l]), w_memkv)
    y_p, ckv_p, krt_p = _prompt_call(
        x_prompt, cs_p, mk, mv, row(g_pre[l]), w_in_ext, ones_blk, g_av_l, w_s[l], b_s_lanes,
        row(g_q[l]), w_uq_ext, row(g_kv[l]), w_ukv, w_out_b, row(g_post[l]))

    x_s = x_sample.reshape(n_seq, D_MODEL)
    z_s, av_s, aout_s, q_lat, q_rope, ckv_s, kr_s, krt_s = _sample_proj_call(
        x_s, cs_s, row(g_pre[l]), w_in_ext, ones_blk, g_av_l, w00, b0, row(g_q[l]), w_uq_ext, row(g_kv[l]), w_ukt)
    pool_krt = jnp.swapaxes(cache_krope[l], 1, 2)
    mem_kt = jnp.transpose(cache_mem_k[l], (0, 2, 3, 1)).reshape(n_seq, C_W, N_MEM)
    mem_vt = jnp.transpose(cache_mem_v[l], (0, 2, 3, 1)).reshape(n_seq, C_W, N_MEM)
    o_lat, c_out = _decode_call(page_table, q_lat, q_rope, ckv_s.reshape(n_seq, 1, KV_LORA),
                                kr_s.reshape(n_seq, 1, QK_ROPE), z_s, mem_kt, mem_vt, cache_ckv[l], pool_krt)
    y_s = _sample_tail_call(x_s, z_s, aout_s, o_lat, c_out.reshape(n_seq, C_W), w_uv_h, w_out_b, row(g_post[l]))

    n_pg = seq // PAGE
    return (
        y_p,
        y_s.reshape(n_seq, dec_seq, D_MODEL),
        ckv_p.reshape(depth, nb, n_pg, PAGE, KV_LORA),
        jnp.swapaxes(krt_p, -1, -2).reshape(depth, nb, n_pg, PAGE, QK_ROPE),
        mk.reshape(depth, nb, N_MEM, C_HEADS, C_HD),
        mv.reshape(depth, nb, N_MEM, C_HEADS, C_HD),
        ckv_s.reshape(depth, n_seq, dec_seq, KV_LORA),
        jnp.transpose(krt_s).reshape(depth, n_seq, dec_seq, QK_ROPE),
        av_s.reshape(depth, n_seq, dec_seq, A_HEADS, A_HD),
    )
```

```python
import functools

import jax
import jax.numpy as jnp
from jax import lax
from jax.experimental import pallas as pl
from jax.experimental.pallas import tpu as pltpu

F32 = jnp.float32
BF16 = jnp.bfloat16

LANES = 128
SUBLANES = 8
VMEM_BYTES_V7X = 64 * 1024 * 1024

D_MODEL = 1024
PAGE = 128
A_HEADS, A_HD = 4, 64
A_W = A_HEADS * A_HD
CHUNK = 128
B_HEADS = 4
V_HEAD = 128
QK_NOPE = 128
QK_ROPE = 64
HALF = QK_ROPE // 2
B_W = B_HEADS * V_HEAD
Q_LORA = 512
KV_LORA = 256
ROPE_BASE = 10000.0
SM_SCALE = (QK_NOPE + QK_ROPE) ** -0.5
N_MEM = 256
C_HEADS, C_HD = 4, 64
C_W = C_HEADS * C_HD
MEM_SCALE = C_HD ** -0.5
EPS = 1e-6
NEG_INF = float("-inf")

OFF_AU = 0
OFF_AV = OFF_AU + A_W
OFF_AG = OFF_AV + A_W
OFF_CQ = OFF_AG + A_W
OFF_CKV = OFF_CQ + Q_LORA
OFF_KR = OFF_CKV + KV_LORA
IN_A = OFF_KR + QK_ROPE
OFF_BG = 0
OFF_CQC = OFF_BG + B_W
OFF_CG = OFF_CQC + C_W
IN_B = OFF_CG + C_W
Q_HEAD_W = QK_NOPE + 2 * QK_ROPE
Q_EXT = B_HEADS * Q_HEAD_W

TQ = 512
DEC_ROWS = SUBLANES
DEC_RP = 64
DEC_SP = 32
DEC_NBUF = 4
DEC_AHEAD = DEC_NBUF - 1
TAIL_SEQ = 8


def _dot(a, b):
    return jnp.dot(a, b, preferred_element_type=F32)


def _dot_nt(a, b):
    return lax.dot_general(a, b, (((1,), (1,)), ((), ())), preferred_element_type=F32)


def _rmsnorm(x, g):
    return (x * lax.rsqrt(jnp.mean(x * x, axis=-1, keepdims=True) + EPS)) * g


def _silu(x):
    return x * (1.0 / (1.0 + jnp.exp(-x)))


def _tile_lanes(x, n):
    return jnp.concatenate([x] * n, axis=-1)


def _lane_head(shape, width):
    assert width & (width - 1) == 0
    return lax.shift_right_logical(lax.broadcasted_iota(jnp.int32, shape, len(shape) - 1), width.bit_length() - 1)


def _group_rmsnorm(v, ones_blk, g):
    sq = v * v
    hi = sq.astype(BF16)
    lo = (sq - hi.astype(F32)).astype(BF16)
    ssum = _dot(hi, ones_blk) + _dot(lo, ones_blk)
    return (v * lax.rsqrt(ssum * (1.0 / A_HD) + EPS)) * g


def _rope128(v, cs):
    w = v * cs
    return w + pltpu.roll(w, QK_ROPE, 1)


def _with_swap(x):
    xx = jnp.concatenate([x, x], axis=-1)
    lane = lax.broadcasted_iota(jnp.int32, xx.shape, 1)
    return jnp.where(lane < QK_ROPE, xx, pltpu.roll(xx, 2 * QK_ROPE - HALF, 1))


def _project(x, cs, g_pre, w_in_a, w_in_b, ones_blk, g_av, g_q, w_uq, g_kv):
    h = _rmsnorm(x, g_pre).astype(BF16)
    za = _dot(h, w_in_a)
    zb = _dot(h, w_in_b)
    a_v = _group_rmsnorm(za[:, OFF_AV:OFF_AV + A_W], ones_blk, g_av)
    cq = _rmsnorm(za[:, OFF_CQ:OFF_CQ + Q_LORA], g_q).astype(BF16)
    q = _dot(cq, w_uq)
    ckv = _rmsnorm(za[:, OFF_CKV:OFF_CKV + KV_LORA], g_kv)
    kr = _rope128(_with_swap(za[:, OFF_KR:OFF_KR + QK_ROPE]), cs)
    return za, zb, a_v, q, ckv, kr


def _softmax_lanes(s):
    m = jnp.max(s, axis=-1, keepdims=True)
    p = jnp.exp(s - m)
    return p / jnp.sum(p, axis=-1, keepdims=True)


def _mem_attn_heads(cq, mkt, mv):
    head = _lane_head(cq.shape, C_HD)
    out = jnp.zeros(cq.shape, F32)
    for hd in range(C_HEADS):
        qm = jnp.where(head == hd, cq, 0.0).astype(BF16)
        p = _softmax_lanes(_dot(qm, mkt) * MEM_SCALE)
        o = _dot(p.astype(BF16), mv)
        out = out + jnp.where(head == hd, o, 0.0)
    return out


def _finish(x, mix, w_out, g_post):
    return x + _rmsnorm(_dot(mix, w_out), g_post)


def _prompt_kernel(x_ref, cs_ref, mem_ref, g_mem_ref, w_memkv_ref, g_pre_ref, w_in_a_ref, w_in_b_ref, ones_ref,
                   g_av_ref, w_s_ref, b_s_ref, g_q_ref, w_uq_ref, g_kv_ref, w_ukv_ref, w_out_ref, g_post_ref,
                   y_ref, ckv_ref, kr_ref, mkt_ref, mvt_ref,
                   k_scr, v_scr, mix_scr, q_scr, mkt_scr, mv_scr, *state_scr):
    m_scr, acc_scr = state_scr[:B_HEADS], state_scr[B_HEADS:]
    i = pl.program_id(1)

    @pl.when(i == 0)
    def _():
        mn = _rmsnorm(mem_ref[...], g_mem_ref[...]).astype(BF16)
        kv = _dot(mn, w_memkv_ref[...])
        mkt = kv[:, :C_W].T
        mkt_ref[...] = mkt
        mvt_ref[...] = kv[:, C_W:].T
        mkt_scr[...] = mkt.astype(BF16)
        mv_scr[...] = kv[:, C_W:].astype(BF16)

    x = x_ref[...]
    cs = cs_ref[...]
    za, zb, a_v, q, ckv, kr = _project(x, cs, g_pre_ref[...], w_in_a_ref[...], w_in_b_ref[...], ones_ref[...],
                                       g_av_ref[...], g_q_ref[...], w_uq_ref[...], g_kv_ref[...])
    ckv_ref[...] = ckv
    for pg in range(TQ // PAGE):
        kr_ref[pg] = kr[pg * PAGE:(pg + 1) * PAGE, :].T[:QK_ROPE, :]

    tril = (lax.broadcasted_iota(jnp.int32, (CHUNK, CHUNK), 1)
            <= lax.broadcasted_iota(jnp.int32, (CHUNK, CHUNK), 0))
    w_stack = jnp.concatenate(
        [jnp.where(tril, w_s_ref[hd], 0.0).astype(BF16) for hd in range(A_HEADS)], axis=0)
    head_a = _lane_head((CHUNK, A_W), A_HD)
    for c in range(TQ // CHUNK):
        rows = slice(c * CHUNK, (c + 1) * CHUNK)
        r = _dot(w_stack, a_v[rows].astype(BF16))
        s = b_s_ref[...]
        for hd in range(A_HEADS):
            s = s + jnp.where(head_a == hd, r[hd * CHUNK:(hd + 1) * CHUNK], 0.0)
        a_out = (za[rows, OFF_AU:OFF_AU + A_W] * s) * _silu(za[rows, OFF_AG:OFF_AG + A_W])
        mix_scr[rows, 0:A_W] = a_out.astype(BF16)

    kv = _dot(ckv.astype(BF16), w_ukv_ref[...])
    lane = lax.broadcasted_iota(jnp.int32, (TQ, 2 * QK_ROPE), 1)
    kr_pad = jnp.where(lane < QK_ROPE, kr, 0.0).astype(BF16)
    row0 = pl.multiple_of(i * TQ, TQ)
    for hd in range(B_HEADS):
        k_scr[hd, pl.ds(row0, TQ), 0:QK_NOPE] = kv[:, hd * QK_NOPE:(hd + 1) * QK_NOPE].astype(BF16)
        k_scr[hd, pl.ds(row0, TQ), QK_NOPE:] = kr_pad
        v_scr[hd, pl.ds(row0, TQ), 0:V_HEAD] = kv[:, B_W + hd * V_HEAD:B_W + (hd + 1) * V_HEAD].astype(BF16)
        v_scr[hd, pl.ds(row0, TQ), V_HEAD:] = jnp.ones((TQ, V_HEAD), BF16)

    for hd in range(B_HEADS):
        qh = q[:, hd * Q_HEAD_W:(hd + 1) * Q_HEAD_W]
        q_scr[hd, :, 0:QK_NOPE] = (qh[:, :QK_NOPE] * SM_SCALE).astype(BF16)
        q_scr[hd, :, QK_NOPE:] = (_rope128(qh[:, QK_NOPE:], cs) * SM_SCALE).astype(BF16)
    for hd in range(B_HEADS):
        m_scr[hd][...] = jnp.full((TQ, LANES), NEG_INF, F32)
        acc_scr[hd][...] = jnp.zeros((TQ, 2 * V_HEAD), F32)

    def attend(rows, k0, n_keys, mask):
        for hd in range(B_HEADS):
            s = _dot_nt(q_scr[hd, rows], k_scr[hd, pl.ds(k0, n_keys), :])
            if mask is not None:
                s = jnp.where(mask, s, NEG_INF)
            m = m_scr[hd][rows]
            m_new = jnp.maximum(m, jnp.broadcast_to(jnp.max(s, axis=-1, keepdims=True), m.shape))
            p = jnp.exp(s - _tile_lanes(m_new, n_keys // LANES)).astype(BF16)
            alpha = _tile_lanes(jnp.exp(m - m_new), 2 * V_HEAD // LANES)
            acc_scr[hd][rows] = alpha * acc_scr[hd][rows] + _dot(p, v_scr[hd, pl.ds(k0, n_keys), :])
            m_scr[hd][rows] = m_new

    def body(j, carry):
        attend(slice(0, TQ), pl.multiple_of(j * TQ, TQ), TQ, None)
        return carry

    lax.fori_loop(0, i, body, 0)
    hq = TQ // 2
    def visible(n_keys, row_offset):
        col = lax.broadcasted_iota(jnp.int32, (hq, n_keys), 1)
        return col <= lax.broadcasted_iota(jnp.int32, (hq, n_keys), 0) + row_offset

    attend(slice(0, hq), row0, hq, visible(hq, 0))
    attend(slice(hq, TQ), row0, TQ, visible(TQ, hq))
    for hd in range(B_HEADS):
        b_g = zb[:, OFF_BG + hd * V_HEAD:OFF_BG + (hd + 1) * V_HEAD]
        acc = acc_scr[hd][...]
        mix_scr[:, A_W + hd * V_HEAD:A_W + (hd + 1) * V_HEAD] = (
            (acc[:, :V_HEAD] / acc[:, V_HEAD:]) * _silu(b_g)).astype(BF16)

    c_out = _mem_attn_heads(zb[:, OFF_CQC:OFF_CQC + C_W], mkt_scr[...], mv_scr[...])
    mix_scr[:, A_W + B_W:] = (c_out * _silu(zb[:, OFF_CG:OFF_CG + C_W])).astype(BF16)

    y_ref[...] = _finish(x, mix_scr[...], w_out_ref[...], g_post_ref[...])


def _full(shape):
    return pl.BlockSpec(shape, lambda *_: (0,) * len(shape))


def _prompt_call(x, cs, mem, g_mem, w_memkv, g_pre, w_in_a, w_in_b, ones_blk, g_av, w_s, b_s, g_q, w_uq, g_kv,
                 w_ukv, w_out, g_post):
    nb, seq, _ = x.shape
    nq = seq // TQ
    weights = (g_mem, w_memkv, g_pre, w_in_a, w_in_b, ones_blk, g_av, w_s, b_s, g_q, w_uq, g_kv, w_ukv, w_out, g_post)
    return pl.pallas_call(
        _prompt_kernel,
        grid=(nb, nq),
        in_specs=[
            pl.BlockSpec((None, TQ, D_MODEL), lambda b, i: (b, i, 0)),
            pl.BlockSpec((TQ, 2 * QK_ROPE), lambda b, i: (i, 0)),
            pl.BlockSpec((None, N_MEM, D_MODEL), lambda b, i: (b, 0, 0)),
        ] + [_full(w.shape) for w in weights],
        out_specs=[
            pl.BlockSpec((None, TQ, D_MODEL), lambda b, i: (b, i, 0)),
            pl.BlockSpec((None, TQ, KV_LORA), lambda b, i: (b, i, 0)),
            pl.BlockSpec((None, TQ // PAGE, QK_ROPE, PAGE), lambda b, i: (b, i, 0, 0)),
            pl.BlockSpec((None, C_W, N_MEM), lambda b, i: (b, 0, 0)),
            pl.BlockSpec((None, C_W, N_MEM), lambda b, i: (b, 0, 0)),
        ],
        out_shape=[
            jax.ShapeDtypeStruct((nb, seq, D_MODEL), F32),
            jax.ShapeDtypeStruct((nb, seq, KV_LORA), F32),
            jax.ShapeDtypeStruct((nb, seq // PAGE, QK_ROPE, PAGE), F32),
            jax.ShapeDtypeStruct((nb, C_W, N_MEM), F32),
            jax.ShapeDtypeStruct((nb, C_W, N_MEM), F32),
        ],
        scratch_shapes=[
            pltpu.VMEM((B_HEADS, seq, QK_NOPE + 2 * QK_ROPE), BF16),
            pltpu.VMEM((B_HEADS, seq, 2 * V_HEAD), BF16),
            pltpu.VMEM((TQ, D_MODEL), BF16),
            pltpu.VMEM((B_HEADS, TQ, QK_NOPE + 2 * QK_ROPE), BF16),
            pltpu.VMEM((C_W, N_MEM), BF16),
            pltpu.VMEM((N_MEM, C_W), BF16),
        ] + [pltpu.VMEM((TQ, LANES), F32)] * B_HEADS + [pltpu.VMEM((TQ, 2 * V_HEAD), F32)] * B_HEADS,
        compiler_params=pltpu.CompilerParams(
            dimension_semantics=("arbitrary", "arbitrary"),
            vmem_limit_bytes=VMEM_BYTES_V7X * 3 // 4),
        name="prompt_layer",
    )(x, cs, mem, *weights)


def _sample_proj_kernel(x_ref, cs_ref, g_pre_ref, w_in_a_ref, w_in_b_ref, ones_ref, g_av_ref, w00_ref, b0_ref,
                        g_q_ref, w_uq_ref, g_kv_ref, w_ukt_ref,
                        zb_ref, av_ref, aout_ref, qlat_ref, qrope_ref, ckv_ref, kr_ref, krt_ref):
    cs = cs_ref[...]
    z, zb, a_v, q, ckv, kr = _project(x_ref[...], cs, g_pre_ref[...], w_in_a_ref[...], w_in_b_ref[...],
                                      ones_ref[...], g_av_ref[...], g_q_ref[...], w_uq_ref[...], g_kv_ref[...])
    zb_ref[...] = zb
    av_ref[...] = a_v
    ckv_ref[...] = ckv
    kr_ref[...] = kr[:, :QK_ROPE]
    krt_ref[...] = kr.T[:QK_ROPE, :]
    s = w00_ref[...] * a_v + b0_ref[...]
    aout_ref[...] = (z[:, OFF_AU:OFF_AU + A_W] * s) * _silu(z[:, OFF_AG:OFF_AG + A_W])
    qlat_ref[...] = jnp.zeros(qlat_ref.shape, F32)
    qrope_ref[...] = jnp.zeros(qrope_ref.shape, F32)
    for hd in range(B_HEADS):
        qh = q[:, hd * Q_HEAD_W:(hd + 1) * Q_HEAD_W]
        qlat_ref[:, hd, :] = _dot(qh[:, :QK_NOPE].astype(BF16), w_ukt_ref[hd])
        qrope_ref[:, hd, :] = _rope128(qh[:, QK_NOPE:], cs)[:, :QK_ROPE]


def _sample_proj_call(x, cs, g_pre, w_in_a, w_in_b, ones_blk, g_av, w00, b0, g_q, w_uq, g_kv, w_ukt):
    n = x.shape[0]
    args = (x, cs, g_pre, w_in_a, w_in_b, ones_blk, g_av, w00, b0, g_q, w_uq, g_kv, w_ukt)
    shapes = ((n, IN_B), (n, A_W), (n, A_W), (n, DEC_ROWS, KV_LORA), (n, DEC_ROWS, QK_ROPE),
              (n, KV_LORA), (n, QK_ROPE), (QK_ROPE, n))
    return pl.pallas_call(
        _sample_proj_kernel,
        grid=(1,),
        in_specs=[_full(a.shape) for a in args],
        out_specs=[_full(s) for s in shapes],
        out_shape=[jax.ShapeDtypeStruct(s, F32) for s in shapes],
        compiler_params=pltpu.CompilerParams(dimension_semantics=("arbitrary",)),
        name="sample_proj",
    )(*args)


def _decode_kernel(pt_ref, qlat_ref, qrope_ref, cnew_ref, knew_ref, z_ref, mkt_ref, mvt_ref, ckv_hbm, krt_hbm,
                   o_ref, c_ref, cbuf, kbuf, sems, *, n_seq, n_regions):
    b = pl.program_id(0)
    n_sub = DEC_RP // DEC_SP

    def page_copies(seq, region, slot, sub):
        out = []
        for k in range(sub * DEC_SP, (sub + 1) * DEC_SP):
            page = pt_ref[seq, region * DEC_RP + k]
            out.append((pltpu.make_async_copy(ckv_hbm.at[page], cbuf.at[slot, k], sems.at[0, slot]), 0))
            out.append((pltpu.make_async_copy(krt_hbm.at[page], kbuf.at[slot, k], sems.at[1, slot]), 1))
        return out

    def ahead_of(r):
        seq = b + (r + DEC_AHEAD) // n_regions
        seq = jnp.where(seq < n_seq, seq, seq - n_seq)
        return seq, (r + DEC_AHEAD) % n_regions, lax.rem(b * n_regions + r + DEC_AHEAD, DEC_NBUF)

    @pl.when(b == 0)
    def _():
        for g in range(DEC_AHEAD):
            for sub in range(n_sub):
                for cp, prio in page_copies(g // n_regions, g % n_regions, g % DEC_NBUF, sub):
                    cp.start(priority=prio)

    q_lat = qlat_ref[...]
    q_rope = qrope_ref[...]
    c_new = cnew_ref[...]
    k_new = knew_ref[...]
    s_new = (jnp.sum(q_lat * c_new, axis=-1, keepdims=True)
             + jnp.sum(q_rope * k_new, axis=-1, keepdims=True)) * SM_SCALE
    q_lat = q_lat.astype(BF16)
    q_rope = q_rope.astype(BF16)

    def mem_attention():
        row = lax.broadcasted_iota(jnp.int32, (SUBLANES, C_W), 0)
        own = _lane_head((SUBLANES, C_W), C_HD) == row
        cq = jnp.broadcast_to(z_ref[pl.ds(b, 1), OFF_CQC:OFF_CQC + C_W], (SUBLANES, C_W))
        p = _softmax_lanes(_dot(jnp.where(own, cq, 0.0).astype(BF16), mkt_ref[...].astype(BF16)) * MEM_SCALE)
        o = _dot_nt(p.astype(BF16), mvt_ref[...].astype(BF16))
        c_ref[...] = jnp.sum(jnp.where(own, o, 0.0), axis=0, keepdims=True)

    parts = []
    for r in range(n_regions):
        slot = lax.rem(b * n_regions + r, DEC_NBUF)
        for sub in range(n_sub):
            for cp, _ in page_copies(b, r, slot, sub):
                cp.wait()
        nxt_seq, nxt_region, nxt_slot = ahead_of(r)
        if r == 0:
            mem_attention()
        for j in range(n_sub):
            ckv_f = cbuf[slot, pl.ds(j * DEC_SP, DEC_SP)].reshape(DEC_SP * PAGE, KV_LORA)
            ckv_j = ckv_f.astype(BF16)
            s_rope = jnp.concatenate(
                [_dot(q_rope, kbuf[slot, j * DEC_SP + k].astype(BF16)) for k in range(DEC_SP)], axis=1)
            s = (_dot(q_lat, ckv_f.T.astype(BF16)) + s_rope) * SM_SCALE
            m_j = jnp.max(s, axis=-1, keepdims=True)
            p = jnp.exp(s - m_j)
            parts.append((m_j, jnp.sum(p, axis=-1, keepdims=True), _dot(p.astype(BF16), ckv_j)))
        for sub in range(n_sub):
            for cp, prio in page_copies(nxt_seq, nxt_region, nxt_slot, sub):
                cp.start(priority=prio)

    m = s_new
    for m_j, _, _ in parts:
        m = jnp.maximum(m, m_j)
    w_new = jnp.exp(s_new - m)
    l = w_new
    acc = w_new * c_new
    for m_j, l_j, acc_j in parts:
        w_j = jnp.exp(m_j - m)
        l = l + w_j * l_j
        acc = acc + w_j * acc_j
    o_ref[...] = acc / l

    @pl.when(b == n_seq - 1)
    def _():
        total = n_seq * n_regions
        for g in range(total, total + DEC_AHEAD):
            for sub in range(n_sub):
                for cp, _ in page_copies(g // n_regions - n_seq, g % n_regions, g % DEC_NBUF, sub):
                    cp.wait()


def _decode_call(page_table, q_lat, q_rope, c_new, k_new, z, mem_kt, mem_vt, pool_ckv, pool_krt):
    n_seq, n_pages = page_table.shape
    assert n_pages % DEC_RP == 0 and DEC_RP % DEC_SP == 0
    n_regions = n_pages // DEC_RP
    assert DEC_AHEAD <= n_regions * n_seq
    grid_spec = pltpu.PrefetchScalarGridSpec(
        num_scalar_prefetch=1,
        grid=(n_seq,),
        in_specs=[
            pl.BlockSpec((None, DEC_ROWS, KV_LORA), lambda b, pt: (b, 0, 0)),
            pl.BlockSpec((None, DEC_ROWS, QK_ROPE), lambda b, pt: (b, 0, 0)),
            pl.BlockSpec((None, 1, KV_LORA), lambda b, pt: (b, 0, 0)),
            pl.BlockSpec((None, 1, QK_ROPE), lambda b, pt: (b, 0, 0)),
            pl.BlockSpec(z.shape, lambda b, pt: (0, 0)),
            pl.BlockSpec((None, C_W, N_MEM), lambda b, pt: (b, 0, 0)),
            pl.BlockSpec((None, C_W, N_MEM), lambda b, pt: (b, 0, 0)),
            pl.BlockSpec(memory_space=pl.ANY),
            pl.BlockSpec(memory_space=pl.ANY),
        ],
        out_specs=[
            pl.BlockSpec((None, DEC_ROWS, KV_LORA), lambda b, pt: (b, 0, 0)),
            pl.BlockSpec((None, 1, C_W), lambda b, pt: (b, 0, 0)),
        ],
        scratch_shapes=[
            pltpu.VMEM((DEC_NBUF, DEC_RP, PAGE, KV_LORA), F32),
            pltpu.VMEM((DEC_NBUF, DEC_RP, QK_ROPE, PAGE), F32),
            pltpu.SemaphoreType.DMA((2, DEC_NBUF)),
        ],
    )
    return pl.pallas_call(
        functools.partial(_decode_kernel, n_seq=n_seq, n_regions=n_regions),
        grid_spec=grid_spec,
        out_shape=[
            jax.ShapeDtypeStruct((n_seq, DEC_ROWS, KV_LORA), F32),
            jax.ShapeDtypeStruct((n_seq, 1, C_W), F32),
        ],
        compiler_params=pltpu.CompilerParams(
            dimension_semantics=("arbitrary",),
            vmem_limit_bytes=VMEM_BYTES_V7X * 7 // 8),
        name="paged_decode",
    )(page_table, q_lat, q_rope, c_new, k_new, z, mem_kt, mem_vt, pool_ckv, pool_krt)


def _sample_tail_kernel(x_ref, z_ref, aout_ref, olat_ref, cout_ref, w_uv_ref, w_out_ref, g_post_ref, y_ref, mix_scr):
    z = z_ref[...]
    mix_scr[:, 0:A_W] = aout_ref[...].astype(BF16)
    for hd in range(B_HEADS):
        o = _dot(olat_ref[:, hd, :].astype(BF16), w_uv_ref[hd])
        b_g = z[:, OFF_BG + hd * V_HEAD:OFF_BG + (hd + 1) * V_HEAD]
        mix_scr[:, A_W + hd * V_HEAD:A_W + (hd + 1) * V_HEAD] = (o * _silu(b_g)).astype(BF16)
    mix_scr[:, A_W + B_W:] = (cout_ref[...] * _silu(z[:, OFF_CG:OFF_CG + C_W])).astype(BF16)
    y_ref[...] = _finish(x_ref[...], mix_scr[...], w_out_ref[...], g_post_ref[...])


def _sample_tail_call(x, z, a_out, o_lat, c_out, w_uv, w_out, g_post):
    n = x.shape[0]
    args = (x, z, a_out, o_lat, c_out, w_uv, w_out, g_post)
    return pl.pallas_call(
        _sample_tail_kernel,
        grid=(1,),
        in_specs=[_full(a.shape) for a in args],
        out_specs=_full((n, D_MODEL)),
        out_shape=jax.ShapeDtypeStruct((n, D_MODEL), F32),
        scratch_shapes=[pltpu.VMEM((n, D_MODEL), BF16)],
        compiler_params=pltpu.CompilerParams(dimension_semantics=("arbitrary",)),
        name="sample_tail",
    )(*args)


def _swap_halves(w):
    return jnp.concatenate([w[..., HALF:], w[..., :HALF]], axis=-1)


def _rope_table(pos):
    inv = ROPE_BASE ** (-jnp.arange(HALF, dtype=F32) / HALF)
    ang = pos[:, None] * inv[None, :]
    cos, sin = jnp.cos(ang), jnp.sin(ang)
    return jnp.concatenate([cos, cos, -sin, sin], axis=-1)


def kernel(x_prompt, x_sample, mem_prompt, cache_ckv, cache_krope, cache_mem_k, cache_mem_v, page_table,
           g_pre, w_in, g_av, w_s, b_s, g_q, w_uq, g_kv, w_uk, w_uv, g_mem, w_mem_k, w_mem_v, w_out, g_post):
    depth = g_pre.shape[0]
    assert depth == 1
    nb, seq, _ = x_prompt.shape
    n_seq, dec_seq, _ = x_sample.shape
    assert dec_seq == 1 and seq % TQ == 0
    past_len = page_table.shape[1] * PAGE
    l = 0

    row = lambda g: g.reshape(1, -1)
    w_in_a = w_in[l][:, :IN_A].astype(BF16)
    w_in_b = w_in[l][:, IN_A:].astype(BF16)
    w_uq_l = w_uq[l]
    w_uq_ext = jnp.concatenate(
        [w_uq_l, _swap_halves(w_uq_l[..., QK_NOPE:])], axis=-1).reshape(Q_LORA, Q_EXT).astype(BF16)
    w_ukv = jnp.concatenate([w_uk[l].reshape(KV_LORA, B_W), w_uv[l].reshape(KV_LORA, B_W)], axis=1).astype(BF16)
    w_ukt = jnp.transpose(w_uk[l], (1, 2, 0)).astype(BF16)
    w_uv_h = jnp.transpose(w_uv[l], (1, 0, 2)).astype(BF16)
    w_out_b = w_out[l].astype(BF16)
    w_memkv = jnp.concatenate(
        [w_mem_k[l].reshape(D_MODEL, C_W), w_mem_v[l].reshape(D_MODEL, C_W)], axis=1).astype(BF16)
    ones_blk = (jnp.arange(A_W)[:, None] // A_HD == jnp.arange(A_W)[None, :] // A_HD).astype(BF16)
    g_av_l = g_av[l].reshape(1, A_W)
    b_s_lanes = jnp.repeat(jnp.transpose(b_s[l]), A_HD, axis=1)
    w00 = jnp.repeat(w_s[l][:, 0, 0], A_HD).reshape(1, A_W)
    b0 = b_s_lanes[0:1]

    cs_p = _rope_table(jnp.arange(seq, dtype=F32))
    cs_s = jnp.broadcast_to(_rope_table(past_len + jnp.arange(dec_seq, dtype=F32)), (n_seq, 2 * QK_ROPE))

    y_p, ckv_p, krt_p, mkt_p, mvt_p = _prompt_call(
        x_prompt, cs_p, mem_prompt, row(g_mem[l]), w_memkv, row(g_pre[l]), w_in_a, w_in_b, ones_blk, g_av_l,
        w_s[l], b_s_lanes, row(g_q[l]), w_uq_ext, row(g_kv[l]), w_ukv, w_out_b, row(g_post[l]))

    x_s = x_sample.reshape(n_seq, D_MODEL)
    z_s, av_s, aout_s, q_lat, q_rope, ckv_s, kr_s, krt_s = _sample_proj_call(
        x_s, cs_s, row(g_pre[l]), w_in_a, w_in_b, ones_blk, g_av_l, w00, b0, row(g_q[l]), w_uq_ext, row(g_kv[l]),
        w_ukt)
    pool_krt = jnp.swapaxes(cache_krope[l], 1, 2)
    mem_kt = jnp.transpose(cache_mem_k[l], (0, 2, 3, 1)).reshape(n_seq, C_W, N_MEM)
    mem_vt = jnp.transpose(cache_mem_v[l], (0, 2, 3, 1)).reshape(n_seq, C_W, N_MEM)
    o_lat, c_out = _decode_call(page_table, q_lat, q_rope, ckv_s.reshape(n_seq, 1, KV_LORA),
                                kr_s.reshape(n_seq, 1, QK_ROPE), z_s, mem_kt, mem_vt, cache_ckv[l], pool_krt)
    y_s = _sample_tail_call(x_s, z_s, aout_s, o_lat, c_out.reshape(n_seq, C_W), w_uv_h, w_out_b, row(g_post[l]))

    n_pg = seq // PAGE
    return (
        y_p,
        y_s.reshape(n_seq, dec_seq, D_MODEL),
        ckv_p.reshape(depth, nb, n_pg, PAGE, KV_LORA),
        jnp.swapaxes(krt_p, -1, -2).reshape(depth, nb, n_pg, PAGE, QK_ROPE),
        jnp.transpose(mkt_p.reshape(depth, nb, C_HEADS, C_HD, N_MEM), (0, 1, 4, 2, 3)),
        jnp.transpose(mvt_p.reshape(depth, nb, C_HEADS, C_HD, N_MEM), (0, 1, 4, 2, 3)),
        ckv_s.reshape(depth, n_seq, dec_seq, KV_LORA),
        jnp.transpose(krt_s).reshape(depth, n_seq, dec_seq, QK_ROPE),
        av_s.reshape(depth, n_seq, dec_seq, A_HEADS, A_HD),
    )
```

```python
import functools

import jax
import jax.numpy as jnp
from jax import lax
from jax.experimental import pallas as pl
from jax.experimental.pallas import tpu as pltpu

F32 = jnp.float32
BF16 = jnp.bfloat16

LANES = 128
SUBLANES = 8
VMEM_BYTES_V7X = 64 * 1024 * 1024

D_MODEL = 1024
PAGE = 128
A_HEADS, A_HD = 4, 64
A_W = A_HEADS * A_HD
CHUNK = 128
B_HEADS = 4
V_HEAD = 128
QK_NOPE = 128
QK_ROPE = 64
HALF = QK_ROPE // 2
B_W = B_HEADS * V_HEAD
Q_LORA = 512
KV_LORA = 256
ROPE_BASE = 10000.0
SM_SCALE = (QK_NOPE + QK_ROPE) ** -0.5
N_MEM = 256
C_HEADS, C_HD = 4, 64
C_W = C_HEADS * C_HD
MEM_SCALE = C_HD ** -0.5
EPS = 1e-6
NEG_INF = float("-inf")

OFF_AU = 0
OFF_AV = OFF_AU + A_W
OFF_AG = OFF_AV + A_W
OFF_CQ = OFF_AG + A_W
OFF_CKV = OFF_CQ + Q_LORA
OFF_KR = OFF_CKV + KV_LORA
IN_A = OFF_KR + 2 * QK_ROPE
OFF_BG = 0
OFF_CQC = OFF_BG + B_W
OFF_CG = OFF_CQC + C_W
IN_B = OFF_CG + C_W
Q_HEAD_W = QK_NOPE + 2 * QK_ROPE
Q_EXT = B_HEADS * Q_HEAD_W

TQ = 512
DEC_ROWS = SUBLANES
DEC_RP = 64
DEC_SP = 32
DEC_NBUF = 4
DEC_AHEAD = DEC_NBUF - 1
TAIL_SEQ = 8


def _dot(a, b):
    return jnp.dot(a, b, preferred_element_type=F32)


def _dot_nt(a, b):
    return lax.dot_general(a, b, (((1,), (1,)), ((), ())), preferred_element_type=F32)


def _rmsnorm(x, g):
    return (x * lax.rsqrt(jnp.mean(x * x, axis=-1, keepdims=True) + EPS)) * g


def _silu(x):
    return x * (1.0 / (1.0 + jnp.exp(-x)))


def _tile_lanes(x, n):
    return jnp.concatenate([x] * n, axis=-1)


def _lane_head(shape, width):
    assert width & (width - 1) == 0
    return lax.shift_right_logical(lax.broadcasted_iota(jnp.int32, shape, len(shape) - 1), width.bit_length() - 1)


def _group_rmsnorm(v, ones_blk, g):
    sq = v * v
    hi = sq.astype(BF16)
    lo = (sq - hi.astype(F32)).astype(BF16)
    ssum = _dot(hi, ones_blk) + _dot(lo, ones_blk)
    return (v * lax.rsqrt(ssum * (1.0 / A_HD) + EPS)) * g


def _rope128(v, cs):
    w = v * cs
    return w + pltpu.roll(w, QK_ROPE, 1)


def _with_swap(x):
    xx = jnp.concatenate([x, x], axis=-1)
    lane = lax.broadcasted_iota(jnp.int32, xx.shape, 1)
    return jnp.where(lane < QK_ROPE, xx, pltpu.roll(xx, 2 * QK_ROPE - HALF, 1))


def _project(x, cs, g_pre, w_in, ones_blk, g_av, g_q, w_uq, g_kv):
    h = _rmsnorm(x, g_pre).astype(BF16)
    z = _dot(h, w_in)
    za, zb = z[:, :IN_A], z[:, IN_A:]
    a_v = _group_rmsnorm(za[:, OFF_AV:OFF_AV + A_W], ones_blk, g_av)
    cq = _rmsnorm(za[:, OFF_CQ:OFF_CQ + Q_LORA], g_q).astype(BF16)
    q = _dot(cq, w_uq)
    ckv = _rmsnorm(za[:, OFF_CKV:OFF_CKV + KV_LORA], g_kv)
    kr = _rope128(za[:, OFF_KR:OFF_KR + 2 * QK_ROPE], cs)
    return za, zb, a_v, q, ckv, kr


W_ROWS = 256


def _w_in_ext_kernel(w_ref, o_ref):
    w = w_ref[...]
    o_ref[:, :OFF_KR] = w[:, :OFF_KR].astype(BF16)
    o_ref[:, OFF_KR:IN_A] = _with_swap(w[:, OFF_KR:OFF_KR + QK_ROPE]).astype(BF16)
    o_ref[:, IN_A:] = w[:, OFF_KR + QK_ROPE:].astype(BF16)


def _w_in_ext_call(w_in):
    rows, cols = w_in.shape
    assert rows % W_ROWS == 0 and cols == IN_A + IN_B - QK_ROPE
    return pl.pallas_call(
        _w_in_ext_kernel,
        grid=(rows // W_ROWS,),
        in_specs=[pl.BlockSpec((W_ROWS, cols), lambda r: (r, 0))],
        out_specs=pl.BlockSpec((W_ROWS, IN_A + IN_B), lambda r: (r, 0)),
        out_shape=jax.ShapeDtypeStruct((rows, IN_A + IN_B), BF16),
        compiler_params=pltpu.CompilerParams(dimension_semantics=("arbitrary",)),
        name="w_in_ext",
    )(w_in)


def _softmax_lanes(s):
    m = jnp.max(s, axis=-1, keepdims=True)
    p = jnp.exp(s - m)
    return p / jnp.sum(p, axis=-1, keepdims=True)


def _mem_attn_heads(cq, mkt, mv):
    head = _lane_head(cq.shape, C_HD)
    out = jnp.zeros(cq.shape, F32)
    for hd in range(C_HEADS):
        qm = jnp.where(head == hd, cq, 0.0).astype(BF16)
        p = _softmax_lanes(_dot(qm, mkt) * MEM_SCALE)
        o = _dot(p.astype(BF16), mv)
        out = out + jnp.where(head == hd, o, 0.0)
    return out


def _finish(x, mix, w_out, g_post):
    return x + _rmsnorm(_dot(mix, w_out), g_post)


def _prompt_kernel(x_ref, cs_ref, mem_ref, g_mem_ref, w_memkv_ref, g_pre_ref, w_in_ref, ones_ref,
                   g_av_ref, w_s_ref, b_s_ref, g_q_ref, w_uq_ref, g_kv_ref, w_ukv_ref, w_out_ref, g_post_ref,
                   y_ref, ckv_ref, kr_ref, mkt_ref, mvt_ref,
                   k_scr, v_scr, mix_scr, q_scr, mkt_scr, mv_scr, *state_scr):
    m_scr, acc_scr = state_scr[:B_HEADS], state_scr[B_HEADS:]
    i = pl.program_id(1)

    @pl.when(i == 0)
    def _():
        mn = _rmsnorm(mem_ref[...], g_mem_ref[...]).astype(BF16)
        kv = _dot(mn, w_memkv_ref[...])
        mkt = kv[:, :C_W].T
        mkt_ref[...] = mkt
        mvt_ref[...] = kv[:, C_W:].T
        mkt_scr[...] = mkt.astype(BF16)
        mv_scr[...] = kv[:, C_W:].astype(BF16)

    x = x_ref[...]
    cs = cs_ref[...]
    za, zb, a_v, q, ckv, kr = _project(x, cs, g_pre_ref[...], w_in_ref[...], ones_ref[...],
                                       g_av_ref[...], g_q_ref[...], w_uq_ref[...], g_kv_ref[...])
    ckv_ref[...] = ckv
    for pg in range(TQ // PAGE):
        kr_ref[pg] = kr[pg * PAGE:(pg + 1) * PAGE, :].T[:QK_ROPE, :]

    tril = (lax.broadcasted_iota(jnp.int32, (CHUNK, CHUNK), 1)
            <= lax.broadcasted_iota(jnp.int32, (CHUNK, CHUNK), 0))
    w_stack = jnp.concatenate(
        [jnp.where(tril, w_s_ref[hd], 0.0).astype(BF16) for hd in range(A_HEADS)], axis=0)
    head_a = _lane_head((CHUNK, A_W), A_HD)
    for c in range(TQ // CHUNK):
        rows = slice(c * CHUNK, (c + 1) * CHUNK)
        r = _dot(w_stack, a_v[rows].astype(BF16))
        s = b_s_ref[...]
        for hd in range(A_HEADS):
            s = s + jnp.where(head_a == hd, r[hd * CHUNK:(hd + 1) * CHUNK], 0.0)
        a_out = (za[rows, OFF_AU:OFF_AU + A_W] * s) * _silu(za[rows, OFF_AG:OFF_AG + A_W])
        mix_scr[rows, 0:A_W] = a_out.astype(BF16)

    kv = _dot(ckv.astype(BF16), w_ukv_ref[...])
    lane = lax.broadcasted_iota(jnp.int32, (TQ, 2 * QK_ROPE), 1)
    kr_pad = jnp.where(lane < QK_ROPE, kr, 0.0).astype(BF16)
    row0 = pl.multiple_of(i * TQ, TQ)
    for hd in range(B_HEADS):
        k_scr[hd, pl.ds(row0, TQ), 0:QK_NOPE] = kv[:, hd * QK_NOPE:(hd + 1) * QK_NOPE].astype(BF16)
        k_scr[hd, pl.ds(row0, TQ), QK_NOPE:] = kr_pad
        v_scr[hd, pl.ds(row0, TQ), 0:V_HEAD] = kv[:, B_W + hd * V_HEAD:B_W + (hd + 1) * V_HEAD].astype(BF16)
        v_scr[hd, pl.ds(row0, TQ), V_HEAD:] = jnp.ones((TQ, V_HEAD), BF16)

    for hd in range(B_HEADS):
        qh = q[:, hd * Q_HEAD_W:(hd + 1) * Q_HEAD_W]
        q_scr[hd, :, 0:QK_NOPE] = (qh[:, :QK_NOPE] * SM_SCALE).astype(BF16)
        q_scr[hd, :, QK_NOPE:] = (_rope128(qh[:, QK_NOPE:], cs) * SM_SCALE).astype(BF16)
    for hd in range(B_HEADS):
        m_scr[hd][...] = jnp.full((TQ, LANES), NEG_INF, F32)
        acc_scr[hd][...] = jnp.zeros((TQ, 2 * V_HEAD), F32)

    def attend(rows, k0, n_keys, mask):
        for hd in range(B_HEADS):
            s = _dot_nt(q_scr[hd, rows], k_scr[hd, pl.ds(k0, n_keys), :])
            if mask is not None:
                s = jnp.where(mask, s, NEG_INF)
            m = m_scr[hd][rows]
            m_new = jnp.maximum(m, jnp.broadcast_to(jnp.max(s, axis=-1, keepdims=True), m.shape))
            p = jnp.exp(s - _tile_lanes(m_new, n_keys // LANES)).astype(BF16)
            alpha = _tile_lanes(jnp.exp(m - m_new), 2 * V_HEAD // LANES)
            acc_scr[hd][rows] = alpha * acc_scr[hd][rows] + _dot(p, v_scr[hd, pl.ds(k0, n_keys), :])
            m_scr[hd][rows] = m_new

    def body(j, carry):
        attend(slice(0, TQ), pl.multiple_of(j * TQ, TQ), TQ, None)
        return carry

    lax.fori_loop(0, i, body, 0)
    hq = TQ // 2
    def visible(n_keys, row_offset):
        col = lax.broadcasted_iota(jnp.int32, (hq, n_keys), 1)
        return col <= lax.broadcasted_iota(jnp.int32, (hq, n_keys), 0) + row_offset

    attend(slice(0, hq), row0, hq, visible(hq, 0))
    attend(slice(hq, TQ), row0, TQ, visible(TQ, hq))
    for hd in range(B_HEADS):
        b_g = zb[:, OFF_BG + hd * V_HEAD:OFF_BG + (hd + 1) * V_HEAD]
        acc = acc_scr[hd][...]
        mix_scr[:, A_W + hd * V_HEAD:A_W + (hd + 1) * V_HEAD] = (
            (acc[:, :V_HEAD] / acc[:, V_HEAD:]) * _silu(b_g)).astype(BF16)

    c_out = _mem_attn_heads(zb[:, OFF_CQC:OFF_CQC + C_W], mkt_scr[...], mv_scr[...])
    mix_scr[:, A_W + B_W:] = (c_out * _silu(zb[:, OFF_CG:OFF_CG + C_W])).astype(BF16)

    y_ref[...] = _finish(x, mix_scr[...], w_out_ref[...], g_post_ref[...])


def _full(shape):
    return pl.BlockSpec(shape, lambda *_: (0,) * len(shape))


def _prompt_call(x, cs, mem, g_mem, w_memkv, g_pre, w_in, ones_blk, g_av, w_s, b_s, g_q, w_uq, g_kv,
                 w_ukv, w_out, g_post):
    nb, seq, _ = x.shape
    nq = seq // TQ
    weights = (g_mem, w_memkv, g_pre, w_in, ones_blk, g_av, w_s, b_s, g_q, w_uq, g_kv, w_ukv, w_out, g_post)
    return pl.pallas_call(
        _prompt_kernel,
        grid=(nb, nq),
        in_specs=[
            pl.BlockSpec((None, TQ, D_MODEL), lambda b, i: (b, i, 0)),
            pl.BlockSpec((TQ, 2 * QK_ROPE), lambda b, i: (i, 0)),
            pl.BlockSpec((None, N_MEM, D_MODEL), lambda b, i: (b, 0, 0)),
        ] + [_full(w.shape) for w in weights],
        out_specs=[
            pl.BlockSpec((None, TQ, D_MODEL), lambda b, i: (b, i, 0)),
            pl.BlockSpec((None, TQ, KV_LORA), lambda b, i: (b, i, 0)),
            pl.BlockSpec((None, TQ // PAGE, QK_ROPE, PAGE), lambda b, i: (b, i, 0, 0)),
            pl.BlockSpec((None, C_W, N_MEM), lambda b, i: (b, 0, 0)),
            pl.BlockSpec((None, C_W, N_MEM), lambda b, i: (b, 0, 0)),
        ],
        out_shape=[
            jax.ShapeDtypeStruct((nb, seq, D_MODEL), F32),
            jax.ShapeDtypeStruct((nb, seq, KV_LORA), F32),
            jax.ShapeDtypeStruct((nb, seq // PAGE, QK_ROPE, PAGE), F32),
            jax.ShapeDtypeStruct((nb, C_W, N_MEM), F32),
            jax.ShapeDtypeStruct((nb, C_W, N_MEM), F32),
        ],
        scratch_shapes=[
            pltpu.VMEM((B_HEADS, seq, QK_NOPE + 2 * QK_ROPE), BF16),
            pltpu.VMEM((B_HEADS, seq, 2 * V_HEAD), BF16),
            pltpu.VMEM((TQ, D_MODEL), BF16),
            pltpu.VMEM((B_HEADS, TQ, QK_NOPE + 2 * QK_ROPE), BF16),
            pltpu.VMEM((C_W, N_MEM), BF16),
            pltpu.VMEM((N_MEM, C_W), BF16),
        ] + [pltpu.VMEM((TQ, LANES), F32)] * B_HEADS + [pltpu.VMEM((TQ, 2 * V_HEAD), F32)] * B_HEADS,
        compiler_params=pltpu.CompilerParams(
            dimension_semantics=("arbitrary", "arbitrary"),
            vmem_limit_bytes=VMEM_BYTES_V7X * 3 // 4),
        name="prompt_layer",
    )(x, cs, mem, *weights)


def _sample_proj_kernel(x_ref, cs_ref, g_pre_ref, w_in_ref, ones_ref, g_av_ref, w00_ref, b0_ref,
                        g_q_ref, w_uq_ref, g_kv_ref, w_ukt_ref,
                        zb_ref, av_ref, aout_ref, qlat_ref, qrope_ref, ckv_ref, kr_ref, krt_ref):
    cs = cs_ref[...]
    z, zb, a_v, q, ckv, kr = _project(x_ref[...], cs, g_pre_ref[...], w_in_ref[...],
                                      ones_ref[...], g_av_ref[...], g_q_ref[...], w_uq_ref[...], g_kv_ref[...])
    zb_ref[...] = zb
    av_ref[...] = a_v
    ckv_ref[...] = ckv
    kr_ref[...] = kr[:, :QK_ROPE]
    krt_ref[...] = kr.T[:QK_ROPE, :]
    s = w00_ref[...] * a_v + b0_ref[...]
    aout_ref[...] = (z[:, OFF_AU:OFF_AU + A_W] * s) * _silu(z[:, OFF_AG:OFF_AG + A_W])
    qlat_ref[...] = jnp.zeros(qlat_ref.shape, F32)
    qrope_ref[...] = jnp.zeros(qrope_ref.shape, F32)
    for hd in range(B_HEADS):
        qh = q[:, hd * Q_HEAD_W:(hd + 1) * Q_HEAD_W]
        qlat_ref[:, hd, :] = _dot(qh[:, :QK_NOPE].astype(BF16), w_ukt_ref[hd])
        qrope_ref[:, hd, :] = _rope128(qh[:, QK_NOPE:], cs)[:, :QK_ROPE]


def _sample_proj_call(x, cs, g_pre, w_in, ones_blk, g_av, w00, b0, g_q, w_uq, g_kv, w_ukt):
    n = x.shape[0]
    args = (x, cs, g_pre, w_in, ones_blk, g_av, w00, b0, g_q, w_uq, g_kv, w_ukt)
    shapes = ((n, IN_B), (n, A_W), (n, A_W), (n, DEC_ROWS, KV_LORA), (n, DEC_ROWS, QK_ROPE),
              (n, KV_LORA), (n, QK_ROPE), (QK_ROPE, n))
    return pl.pallas_call(
        _sample_proj_kernel,
        grid=(1,),
        in_specs=[_full(a.shape) for a in args],
        out_specs=[_full(s) for s in shapes],
        out_shape=[jax.ShapeDtypeStruct(s, F32) for s in shapes],
        compiler_params=pltpu.CompilerParams(dimension_semantics=("arbitrary",)),
        name="sample_proj",
    )(*args)


def _decode_kernel(pt_ref, qlat_ref, qrope_ref, cnew_ref, knew_ref, z_ref, mkt_ref, mvt_ref, ckv_hbm, krt_hbm,
                   o_ref, c_ref, cbuf, kbuf, sems, *, n_seq, n_regions):
    b = pl.program_id(0)
    n_sub = DEC_RP // DEC_SP

    def page_copies(seq, region, slot, sub):
        out = []
        for k in range(sub * DEC_SP, (sub + 1) * DEC_SP):
            page = pt_ref[seq, region * DEC_RP + k]
            out.append(pltpu.make_async_copy(ckv_hbm.at[page], cbuf.at[slot, k], sems.at[0, slot]))
            out.append(pltpu.make_async_copy(krt_hbm.at[page], kbuf.at[slot, k], sems.at[1, slot]))
        return out

    def ahead_of(r):
        seq = b + (r + DEC_AHEAD) // n_regions
        seq = jnp.where(seq < n_seq, seq, seq - n_seq)
        return seq, (r + DEC_AHEAD) % n_regions, lax.rem(b * n_regions + r + DEC_AHEAD, DEC_NBUF)

    @pl.when(b == 0)
    def _():
        for g in range(DEC_AHEAD):
            for sub in range(n_sub):
                for cp in page_copies(g // n_regions, g % n_regions, g % DEC_NBUF, sub):
                    cp.start()

    q_lat = qlat_ref[...]
    q_rope = qrope_ref[...]
    c_new = cnew_ref[...]
    k_new = knew_ref[...]
    s_new = (jnp.sum(q_lat * c_new, axis=-1, keepdims=True)
             + jnp.sum(q_rope * k_new, axis=-1, keepdims=True)) * SM_SCALE
    q_lat = q_lat.astype(BF16)
    q_rope = q_rope.astype(BF16)

    def mem_attention():
        row = lax.broadcasted_iota(jnp.int32, (SUBLANES, C_W), 0)
        own = _lane_head((SUBLANES, C_W), C_HD) == row
        cq = jnp.broadcast_to(z_ref[pl.ds(b, 1), OFF_CQC:OFF_CQC + C_W], (SUBLANES, C_W))
        p = _softmax_lanes(_dot(jnp.where(own, cq, 0.0).astype(BF16), mkt_ref[...].astype(BF16)) * MEM_SCALE)
        o = _dot_nt(p.astype(BF16), mvt_ref[...].astype(BF16))
        c_ref[...] = jnp.sum(jnp.where(own, o, 0.0), axis=0, keepdims=True)

    parts = []
    for r in range(n_regions):
        slot = lax.rem(b * n_regions + r, DEC_NBUF)
        for sub in range(n_sub):
            for cp in page_copies(b, r, slot, sub):
                cp.wait()
        nxt_seq, nxt_region, nxt_slot = ahead_of(r)
        if r == 0:
            mem_attention()
        for j in range(n_sub):
            ckv_f = cbuf[slot, pl.ds(j * DEC_SP, DEC_SP)].reshape(DEC_SP * PAGE, KV_LORA)
            ckv_j = ckv_f.astype(BF16)
            s_rope = jnp.concatenate(
                [_dot(q_rope, kbuf[slot, j * DEC_SP + k].astype(BF16)) for k in range(DEC_SP)], axis=1)
            s = (_dot(q_lat, ckv_f.T.astype(BF16)) + s_rope) * SM_SCALE
            m_j = jnp.max(s, axis=-1, keepdims=True)
            p = jnp.exp(s - m_j)
            parts.append((m_j, jnp.sum(p, axis=-1, keepdims=True), _dot(p.astype(BF16), ckv_j)))
        for sub in range(n_sub):
            for cp in page_copies(nxt_seq, nxt_region, nxt_slot, sub):
                cp.start()

    m = s_new
    for m_j, _, _ in parts:
        m = jnp.maximum(m, m_j)
    w_new = jnp.exp(s_new - m)
    l = w_new
    acc = w_new * c_new
    for m_j, l_j, acc_j in parts:
        w_j = jnp.exp(m_j - m)
        l = l + w_j * l_j
        acc = acc + w_j * acc_j
    o_ref[...] = acc / l

    @pl.when(b == n_seq - 1)
    def _():
        total = n_seq * n_regions
        for g in range(total, total + DEC_AHEAD):
            for sub in range(n_sub):
                for cp in page_copies(g // n_regions - n_seq, g % n_regions, g % DEC_NBUF, sub):
                    cp.wait()


def _decode_call(page_table, q_lat, q_rope, c_new, k_new, z, mem_kt, mem_vt, pool_ckv, pool_krt):
    n_seq, n_pages = page_table.shape
    assert n_pages % DEC_RP == 0 and DEC_RP % DEC_SP == 0
    n_regions = n_pages // DEC_RP
    assert DEC_AHEAD <= n_regions * n_seq
    grid_spec = pltpu.PrefetchScalarGridSpec(
        num_scalar_prefetch=1,
        grid=(n_seq,),
        in_specs=[
            pl.BlockSpec((None, DEC_ROWS, KV_LORA), lambda b, pt: (b, 0, 0)),
            pl.BlockSpec((None, DEC_ROWS, QK_ROPE), lambda b, pt: (b, 0, 0)),
            pl.BlockSpec((None, 1, KV_LORA), lambda b, pt: (b, 0, 0)),
            pl.BlockSpec((None, 1, QK_ROPE), lambda b, pt: (b, 0, 0)),
            pl.BlockSpec(z.shape, lambda b, pt: (0, 0)),
            pl.BlockSpec((None, C_W, N_MEM), lambda b, pt: (b, 0, 0)),
            pl.BlockSpec((None, C_W, N_MEM), lambda b, pt: (b, 0, 0)),
            pl.BlockSpec(memory_space=pl.ANY),
            pl.BlockSpec(memory_space=pl.ANY),
        ],
        out_specs=[
            pl.BlockSpec((None, DEC_ROWS, KV_LORA), lambda b, pt: (b, 0, 0)),
            pl.BlockSpec((None, 1, C_W), lambda b, pt: (b, 0, 0)),
        ],
        scratch_shapes=[
            pltpu.VMEM((DEC_NBUF, DEC_RP, PAGE, KV_LORA), F32),
            pltpu.VMEM((DEC_NBUF, DEC_RP, QK_ROPE, PAGE), F32),
            pltpu.SemaphoreType.DMA((2, DEC_NBUF)),
        ],
    )
    return pl.pallas_call(
        functools.partial(_decode_kernel, n_seq=n_seq, n_regions=n_regions),
        grid_spec=grid_spec,
        out_shape=[
            jax.ShapeDtypeStruct((n_seq, DEC_ROWS, KV_LORA), F32),
            jax.ShapeDtypeStruct((n_seq, 1, C_W), F32),
        ],
        compiler_params=pltpu.CompilerParams(
            dimension_semantics=("arbitrary",),
            vmem_limit_bytes=VMEM_BYTES_V7X * 7 // 8),
        name="paged_decode",
    )(page_table, q_lat, q_rope, c_new, k_new, z, mem_kt, mem_vt, pool_ckv, pool_krt)


def _sample_tail_kernel(x_ref, z_ref, aout_ref, olat_ref, cout_ref, w_uv_ref, w_out_ref, g_post_ref, y_ref, mix_scr):
    z = z_ref[...]
    mix_scr[:, 0:A_W] = aout_ref[...].astype(BF16)
    for hd in range(B_HEADS):
        o = _dot(olat_ref[:, hd, :].astype(BF16), w_uv_ref[hd])
        b_g = z[:, OFF_BG + hd * V_HEAD:OFF_BG + (hd + 1) * V_HEAD]
        mix_scr[:, A_W + hd * V_HEAD:A_W + (hd + 1) * V_HEAD] = (o * _silu(b_g)).astype(BF16)
    mix_scr[:, A_W + B_W:] = (cout_ref[...] * _silu(z[:, OFF_CG:OFF_CG + C_W])).astype(BF16)
    y_ref[...] = _finish(x_ref[...], mix_scr[...], w_out_ref[...], g_post_ref[...])


def _sample_tail_call(x, z, a_out, o_lat, c_out, w_uv, w_out, g_post):
    n = x.shape[0]
    args = (x, z, a_out, o_lat, c_out, w_uv, w_out, g_post)
    return pl.pallas_call(
        _sample_tail_kernel,
        grid=(1,),
        in_specs=[_full(a.shape) for a in args],
        out_specs=_full((n, D_MODEL)),
        out_shape=jax.ShapeDtypeStruct((n, D_MODEL), F32),
        scratch_shapes=[pltpu.VMEM((n, D_MODEL), BF16)],
        compiler_params=pltpu.CompilerParams(dimension_semantics=("arbitrary",)),
        name="sample_tail",
    )(*args)


def _swap_halves(w):
    return jnp.concatenate([w[..., HALF:], w[..., :HALF]], axis=-1)


def _rope_table(pos):
    inv = ROPE_BASE ** (-jnp.arange(HALF, dtype=F32) / HALF)
    ang = pos[:, None] * inv[None, :]
    cos, sin = jnp.cos(ang), jnp.sin(ang)
    return jnp.concatenate([cos, cos, -sin, sin], axis=-1)


def kernel(x_prompt, x_sample, mem_prompt, cache_ckv, cache_krope, cache_mem_k, cache_mem_v, page_table,
           g_pre, w_in, g_av, w_s, b_s, g_q, w_uq, g_kv, w_uk, w_uv, g_mem, w_mem_k, w_mem_v, w_out, g_post):
    depth = g_pre.shape[0]
    assert depth == 1
    nb, seq, _ = x_prompt.shape
    n_seq, dec_seq, _ = x_sample.shape
    assert dec_seq == 1 and seq % TQ == 0
    past_len = page_table.shape[1] * PAGE
    l = 0

    row = lambda g: g.reshape(1, -1)
    w_in_ext = _w_in_ext_call(w_in[l])
    w_uq_l = w_uq[l]
    w_uq_ext = jnp.concatenate(
        [w_uq_l, _swap_halves(w_uq_l[..., QK_NOPE:])], axis=-1).reshape(Q_LORA, Q_EXT).astype(BF16)
    w_ukv = jnp.concatenate([w_uk[l].reshape(KV_LORA, B_W), w_uv[l].reshape(KV_LORA, B_W)], axis=1).astype(BF16)
    w_ukt = jnp.transpose(w_uk[l], (1, 2, 0)).astype(BF16)
    w_uv_h = jnp.transpose(w_uv[l], (1, 0, 2)).astype(BF16)
    w_out_b = w_out[l].astype(BF16)
    w_memkv = jnp.concatenate(
        [w_mem_k[l].reshape(D_MODEL, C_W), w_mem_v[l].reshape(D_MODEL, C_W)], axis=1).astype(BF16)
    ones_blk = (jnp.arange(A_W)[:, None] // A_HD == jnp.arange(A_W)[None, :] // A_HD).astype(BF16)
    g_av_l = g_av[l].reshape(1, A_W)
    b_s_lanes = jnp.repeat(jnp.transpose(b_s[l]), A_HD, axis=1)
    w00 = jnp.repeat(w_s[l][:, 0, 0], A_HD).reshape(1, A_W)
    b0 = b_s_lanes[0:1]

    cs_p = _rope_table(jnp.arange(seq, dtype=F32))
    cs_s = jnp.broadcast_to(_rope_table(past_len + jnp.arange(dec_seq, dtype=F32)), (n_seq, 2 * QK_ROPE))

    y_p, ckv_p, krt_p, mkt_p, mvt_p = _prompt_call(
        x_prompt, cs_p, mem_prompt, row(g_mem[l]), w_memkv, row(g_pre[l]), w_in_ext, ones_blk, g_av_l,
        w_s[l], b_s_lanes, row(g_q[l]), w_uq_ext, row(g_kv[l]), w_ukv, w_out_b, row(g_post[l]))

    x_s = x_sample.reshape(n_seq, D_MODEL)
    z_s, av_s, aout_s, q_lat, q_rope, ckv_s, kr_s, krt_s = _sample_proj_call(
        x_s, cs_s, row(g_pre[l]), w_in_ext, ones_blk, g_av_l, w00, b0, row(g_q[l]), w_uq_ext, row(g_kv[l]), w_ukt)
    pool_krt = jnp.swapaxes(cache_krope[l], 1, 2)
    mem_kt = jnp.transpose(cache_mem_k[l], (0, 2, 3, 1)).reshape(n_seq, C_W, N_MEM)
    mem_vt = jnp.transpose(cache_mem_v[l], (0, 2, 3, 1)).reshape(n_seq, C_W, N_MEM)
    o_lat, c_out = _decode_call(page_table, q_lat, q_rope, ckv_s.reshape(n_seq, 1, KV_LORA),
                                kr_s.reshape(n_seq, 1, QK_ROPE), z_s, mem_kt, mem_vt, cache_ckv[l], pool_krt)
    y_s = _sample_tail_call(x_s, z_s, aout_s, o_lat, c_out.reshape(n_seq, C_W), w_uv_h, w_out_b, row(g_post[l]))

    n_pg = seq // PAGE
    return (
        y_p,
        y_s.reshape(n_seq, dec_seq, D_MODEL),
        ckv_p.reshape(depth, nb, n_pg, PAGE, KV_LORA),
        jnp.swapaxes(krt_p, -1, -2).reshape(depth, nb, n_pg, PAGE, QK_ROPE),
        jnp.transpose(mkt_p.reshape(depth, nb, C_HEADS, C_HD, N_MEM), (0, 1, 4, 2, 3)),
        jnp.transpose(mvt_p.reshape(depth, nb, C_HEADS, C_HD, N_MEM), (0, 1, 4, 2, 3)),
        ckv_s.reshape(depth, n_seq, dec_seq, KV_LORA),
        jnp.transpose(krt_s).reshape(depth, n_seq, dec_seq, QK_ROPE),
        av_s.reshape(depth, n_seq, dec_seq, A_HEADS, A_HD),
    )
```

```python
import functools

import jax
import jax.numpy as jnp
from jax import lax
from jax.experimental import pallas as pl
from jax.experimental.pallas import tpu as pltpu

F32 = jnp.float32
BF16 = jnp.bfloat16

LANES = 128
SUBLANES = 8
VMEM_BYTES_V7X = 64 * 1024 * 1024

D_MODEL = 1024
PAGE = 128
A_HEADS, A_HD = 4, 64
A_W = A_HEADS * A_HD
CHUNK = 128
B_HEADS = 4
V_HEAD = 128
QK_NOPE = 128
QK_ROPE = 64
HALF = QK_ROPE // 2
B_W = B_HEADS * V_HEAD
Q_LORA = 512
KV_LORA = 256
ROPE_BASE = 10000.0
SM_SCALE = (QK_NOPE + QK_ROPE) ** -0.5
N_MEM = 256
C_HEADS, C_HD = 4, 64
C_W = C_HEADS * C_HD
MEM_SCALE = C_HD ** -0.5
EPS = 1e-6
NEG_INF = float("-inf")

OFF_AU = 0
OFF_AV = OFF_AU + A_W
OFF_AG = OFF_AV + A_W
OFF_CQ = OFF_AG + A_W
OFF_CKV = OFF_CQ + Q_LORA
OFF_KR = OFF_CKV + KV_LORA
IN_A = OFF_KR + 2 * QK_ROPE
OFF_BG = 0
OFF_CQC = OFF_BG + B_W
OFF_CG = OFF_CQC + C_W
IN_B = OFF_CG + C_W
Q_HEAD_W = QK_NOPE + 2 * QK_ROPE
Q_EXT = B_HEADS * Q_HEAD_W

TQ = 512
DEC_ROWS = SUBLANES
DEC_RP = 64
DEC_SP = 32
DEC_NBUF = 4
DEC_AHEAD = DEC_NBUF - 1
PROMPT_VMEM_LIMIT = VMEM_BYTES_V7X * 3 // 4
DECODE_VMEM_LIMIT = VMEM_BYTES_V7X * 7 // 8


def _dot(a, b):
    return jnp.dot(a, b, preferred_element_type=F32)


def _dot_nt(a, b):
    return lax.dot_general(a, b, (((1,), (1,)), ((), ())), preferred_element_type=F32)


def _rmsnorm(x, g):
    return (x * lax.rsqrt(jnp.mean(x * x, axis=-1, keepdims=True) + EPS)) * g


def _silu(x):
    return x * (1.0 / (1.0 + jnp.exp(-x)))


def _tile_lanes(x, n):
    return jnp.concatenate([x] * n, axis=-1)


def _lane_head(shape, width):
    assert width & (width - 1) == 0
    return lax.shift_right_logical(lax.broadcasted_iota(jnp.int32, shape, len(shape) - 1), width.bit_length() - 1)


def _group_rmsnorm(v, ones_blk, g):
    sq = v * v
    hi = sq.astype(BF16)
    lo = (sq - hi.astype(F32)).astype(BF16)
    ssum = _dot(hi, ones_blk) + _dot(lo, ones_blk)
    return (v * lax.rsqrt(ssum * (1.0 / A_HD) + EPS)) * g


def _rope128(v, cs):
    w = v * cs
    return w + pltpu.roll(w, QK_ROPE, 1)


def _project(x, cs, g_pre, w_in_t, ones_blk, g_av, g_q, w_uq_t, g_kv):
    h = _rmsnorm(x, g_pre).astype(BF16)
    z = _dot_nt(h, w_in_t)
    za, zb = z[:, :IN_A], z[:, IN_A:]
    a_v = _group_rmsnorm(za[:, OFF_AV:OFF_AV + A_W], ones_blk, g_av)
    cq = _rmsnorm(za[:, OFF_CQ:OFF_CQ + Q_LORA], g_q).astype(BF16)
    q = _dot_nt(cq, w_uq_t)
    ckv = _rmsnorm(za[:, OFF_CKV:OFF_CKV + KV_LORA], g_kv)
    kr = _rope128(za[:, OFF_KR:OFF_KR + 2 * QK_ROPE], cs)
    return za, zb, a_v, q, ckv, kr


W_COL_BLOCKS = 4


def _weight_prep_kernel(w_in_ref, w_uq_ref, w_mk_ref, w_mv_ref, o_in_ref, o_uq_ref, o_mem_ref):
    def put(o_ref, dst, w_ref, src, n):
        o_ref[dst:dst + n] = w_ref[src:src + n].astype(BF16)

    rope_end = OFF_KR + QK_ROPE
    put(o_in_ref, 0, w_in_ref, 0, rope_end)
    put(o_in_ref, rope_end, w_in_ref, OFF_KR + HALF, HALF)
    put(o_in_ref, rope_end + HALF, w_in_ref, OFF_KR, HALF)
    put(o_in_ref, IN_A, w_in_ref, rope_end, IN_B)
    head_in = QK_NOPE + QK_ROPE
    for hd in range(B_HEADS):
        src, dst = hd * head_in, hd * Q_HEAD_W
        put(o_uq_ref, dst, w_uq_ref, src, head_in)
        put(o_uq_ref, dst + head_in, w_uq_ref, src + QK_NOPE + HALF, HALF)
        put(o_uq_ref, dst + head_in + HALF, w_uq_ref, src + QK_NOPE, HALF)
    put(o_mem_ref, 0, w_mk_ref, 0, C_W)
    put(o_mem_ref, C_W, w_mv_ref, 0, C_W)


def _weight_prep_call(w_in_t, w_uq_t, w_mk_t, w_mv_t):
    assert w_in_t.shape == (IN_A + IN_B - QK_ROPE, D_MODEL) and w_uq_t.shape == (B_HEADS * (QK_NOPE + QK_ROPE), Q_LORA)
    outs = ((IN_A + IN_B, D_MODEL), (Q_EXT, Q_LORA), (2 * C_W, D_MODEL))

    def col_blocks(shape):
        return pl.BlockSpec((shape[0], shape[1] // W_COL_BLOCKS), lambda c: (0, c))

    args = (w_in_t, w_uq_t, w_mk_t, w_mv_t)
    return pl.pallas_call(
        _weight_prep_kernel,
        grid=(W_COL_BLOCKS,),
        in_specs=[col_blocks(a.shape) for a in args],
        out_specs=[col_blocks(s) for s in outs],
        out_shape=[jax.ShapeDtypeStruct(s, BF16) for s in outs],
        compiler_params=pltpu.CompilerParams(dimension_semantics=("arbitrary",)),
        name="weight_prep",
    )(*args)


def _softmax_lanes(s):
    m = jnp.max(s, axis=-1, keepdims=True)
    p = jnp.exp(s - m)
    return p / jnp.sum(p, axis=-1, keepdims=True)


def _mem_attn_heads(cq, mkt, mv):
    head = _lane_head(cq.shape, C_HD)
    out = jnp.zeros(cq.shape, F32)
    for hd in range(C_HEADS):
        qm = jnp.where(head == hd, cq, 0.0).astype(BF16)
        p = _softmax_lanes(_dot(qm, mkt) * MEM_SCALE)
        o = _dot(p.astype(BF16), mv)
        out = out + jnp.where(head == hd, o, 0.0)
    return out


def _finish(x, mix, w_out, g_post):
    return x + _rmsnorm(_dot(mix, w_out), g_post)


def _prompt_kernel(x_ref, cs_ref, mem_ref, g_mem_ref, w_memkv_ref, g_pre_ref, w_in_ref, ones_ref,
                   g_av_ref, w_s_ref, b_s_ref, g_q_ref, w_uq_ref, g_kv_ref, w_ukv_ref, w_out_ref, g_post_ref,
                   y_ref, ckv_ref, kr_ref, mkt_ref, mvt_ref,
                   k_scr, v_scr, mix_scr, q_scr, mkt_scr, mv_scr, *state_scr):
    m_scr, acc_scr = state_scr[:B_HEADS], state_scr[B_HEADS:]
    i = pl.program_id(1)

    @pl.when(i == 0)
    def _():
        mn = _rmsnorm(mem_ref[...], g_mem_ref[...]).astype(BF16)
        kv = _dot_nt(mn, w_memkv_ref[...])
        mkt = kv[:, :C_W].T
        mkt_ref[...] = mkt
        mvt_ref[...] = kv[:, C_W:].T
        mkt_scr[...] = mkt.astype(BF16)
        mv_scr[...] = kv[:, C_W:].astype(BF16)

    x = x_ref[...]
    cs = cs_ref[...]
    za, zb, a_v, q, ckv, kr = _project(x, cs, g_pre_ref[...], w_in_ref[...], ones_ref[...],
                                       g_av_ref[...], g_q_ref[...], w_uq_ref[...], g_kv_ref[...])
    ckv_ref[...] = ckv
    for pg in range(TQ // PAGE):
        kr_ref[pg] = kr[pg * PAGE:(pg + 1) * PAGE, :].T[:QK_ROPE, :]

    tril = (lax.broadcasted_iota(jnp.int32, (CHUNK, CHUNK), 1)
            <= lax.broadcasted_iota(jnp.int32, (CHUNK, CHUNK), 0))
    w_stack = jnp.concatenate(
        [jnp.where(tril, w_s_ref[hd], 0.0).astype(BF16) for hd in range(A_HEADS)], axis=0)
    head_a = _lane_head((CHUNK, A_W), A_HD)
    for c in range(TQ // CHUNK):
        rows = slice(c * CHUNK, (c + 1) * CHUNK)
        r = _dot(w_stack, a_v[rows].astype(BF16))
        s = b_s_ref[...]
        for hd in range(A_HEADS):
            s = s + jnp.where(head_a == hd, r[hd * CHUNK:(hd + 1) * CHUNK], 0.0)
        a_out = (za[rows, OFF_AU:OFF_AU + A_W] * s) * _silu(za[rows, OFF_AG:OFF_AG + A_W])
        mix_scr[rows, 0:A_W] = a_out.astype(BF16)

    kv = _dot(ckv.astype(BF16), w_ukv_ref[...])
    lane = lax.broadcasted_iota(jnp.int32, (TQ, 2 * QK_ROPE), 1)
    kr_pad = jnp.where(lane < QK_ROPE, kr, 0.0).astype(BF16)
    row0 = pl.multiple_of(i * TQ, TQ)
    for hd in range(B_HEADS):
        k_scr[hd, pl.ds(row0, TQ), 0:QK_NOPE] = kv[:, hd * QK_NOPE:(hd + 1) * QK_NOPE].astype(BF16)
        k_scr[hd, pl.ds(row0, TQ), QK_NOPE:] = kr_pad
        v_scr[hd, pl.ds(row0, TQ), 0:V_HEAD] = kv[:, B_W + hd * V_HEAD:B_W + (hd + 1) * V_HEAD].astype(BF16)
        v_scr[hd, pl.ds(row0, TQ), V_HEAD:] = jnp.ones((TQ, V_HEAD), BF16)

    for hd in range(B_HEADS):
        qh = q[:, hd * Q_HEAD_W:(hd + 1) * Q_HEAD_W]
        q_scr[hd, :, 0:QK_NOPE] = (qh[:, :QK_NOPE] * SM_SCALE).astype(BF16)
        q_scr[hd, :, QK_NOPE:] = (_rope128(qh[:, QK_NOPE:], cs) * SM_SCALE).astype(BF16)
    for hd in range(B_HEADS):
        m_scr[hd][...] = jnp.full((TQ, LANES), NEG_INF, F32)
        acc_scr[hd][...] = jnp.zeros((TQ, 2 * V_HEAD), F32)

    def attend(rows, k0, n_keys, mask):
        for hd in range(B_HEADS):
            s = _dot_nt(q_scr[hd, rows], k_scr[hd, pl.ds(k0, n_keys), :])
            if mask is not None:
                s = jnp.where(mask, s, NEG_INF)
            m = m_scr[hd][rows]
            m_new = jnp.maximum(m, jnp.broadcast_to(jnp.max(s, axis=-1, keepdims=True), m.shape))
            p = jnp.exp(s - _tile_lanes(m_new, n_keys // LANES)).astype(BF16)
            alpha = _tile_lanes(jnp.exp(m - m_new), 2 * V_HEAD // LANES)
            acc_scr[hd][rows] = alpha * acc_scr[hd][rows] + _dot(p, v_scr[hd, pl.ds(k0, n_keys), :])
            m_scr[hd][rows] = m_new

    def body(j, carry):
        attend(slice(0, TQ), pl.multiple_of(j * TQ, TQ), TQ, None)
        return carry

    lax.fori_loop(0, i, body, 0)
    hq = TQ // 2
    def visible(n_keys, row_offset):
        col = lax.broadcasted_iota(jnp.int32, (hq, n_keys), 1)
        return col <= lax.broadcasted_iota(jnp.int32, (hq, n_keys), 0) + row_offset

    attend(slice(0, hq), row0, hq, visible(hq, 0))
    attend(slice(hq, TQ), row0, TQ, visible(TQ, hq))
    for hd in range(B_HEADS):
        b_g = zb[:, OFF_BG + hd * V_HEAD:OFF_BG + (hd + 1) * V_HEAD]
        acc = acc_scr[hd][...]
        mix_scr[:, A_W + hd * V_HEAD:A_W + (hd + 1) * V_HEAD] = (
            (acc[:, :V_HEAD] / acc[:, V_HEAD:]) * _silu(b_g)).astype(BF16)

    c_out = _mem_attn_heads(zb[:, OFF_CQC:OFF_CQC + C_W], mkt_scr[...], mv_scr[...])
    mix_scr[:, A_W + B_W:] = (c_out * _silu(zb[:, OFF_CG:OFF_CG + C_W])).astype(BF16)

    y_ref[...] = _finish(x, mix_scr[...], w_out_ref[...], g_post_ref[...])


def _full(shape):
    return pl.BlockSpec(shape, lambda *_: (0,) * len(shape))


def _prompt_call(x, cs, mem, g_mem, w_memkv, g_pre, w_in, ones_blk, g_av, w_s, b_s, g_q, w_uq, g_kv,
                 w_ukv, w_out, g_post):
    nb, seq, _ = x.shape
    nq = seq // TQ
    weights = (g_mem, w_memkv, g_pre, w_in, ones_blk, g_av, w_s, b_s, g_q, w_uq, g_kv, w_ukv, w_out, g_post)
    return pl.pallas_call(
        _prompt_kernel,
        grid=(nb, nq),
        in_specs=[
            pl.BlockSpec((None, TQ, D_MODEL), lambda b, i: (b, i, 0)),
            pl.BlockSpec((TQ, 2 * QK_ROPE), lambda b, i: (i, 0)),
            pl.BlockSpec((None, N_MEM, D_MODEL), lambda b, i: (b, 0, 0)),
        ] + [_full(w.shape) for w in weights],
        out_specs=[
            pl.BlockSpec((None, TQ, D_MODEL), lambda b, i: (b, i, 0)),
            pl.BlockSpec((None, TQ, KV_LORA), lambda b, i: (b, i, 0)),
            pl.BlockSpec((None, TQ // PAGE, QK_ROPE, PAGE), lambda b, i: (b, i, 0, 0)),
            pl.BlockSpec((None, C_W, N_MEM), lambda b, i: (b, 0, 0)),
            pl.BlockSpec((None, C_W, N_MEM), lambda b, i: (b, 0, 0)),
        ],
        out_shape=[
            jax.ShapeDtypeStruct((nb, seq, D_MODEL), F32),
            jax.ShapeDtypeStruct((nb, seq, KV_LORA), F32),
            jax.ShapeDtypeStruct((nb, seq // PAGE, QK_ROPE, PAGE), F32),
            jax.ShapeDtypeStruct((nb, C_W, N_MEM), F32),
            jax.ShapeDtypeStruct((nb, C_W, N_MEM), F32),
        ],
        scratch_shapes=[
            pltpu.VMEM((B_HEADS, seq, QK_NOPE + 2 * QK_ROPE), BF16),
            pltpu.VMEM((B_HEADS, seq, 2 * V_HEAD), BF16),
            pltpu.VMEM((TQ, D_MODEL), BF16),
            pltpu.VMEM((B_HEADS, TQ, QK_NOPE + 2 * QK_ROPE), BF16),
            pltpu.VMEM((C_W, N_MEM), BF16),
            pltpu.VMEM((N_MEM, C_W), BF16),
        ] + [pltpu.VMEM((TQ, LANES), F32)] * B_HEADS + [pltpu.VMEM((TQ, 2 * V_HEAD), F32)] * B_HEADS,
        compiler_params=pltpu.CompilerParams(
            dimension_semantics=("arbitrary", "arbitrary"),
            vmem_limit_bytes=PROMPT_VMEM_LIMIT),
        name="prompt_layer",
    )(x, cs, mem, *weights)


def _sample_proj_kernel(x_ref, cs_ref, g_pre_ref, w_in_ref, ones_ref, g_av_ref, w00_ref, b0_ref,
                        g_q_ref, w_uq_ref, g_kv_ref, w_ukt_ref,
                        zb_ref, av_ref, aout_ref, qlat_ref, qrope_ref, ckv_ref, kr_ref, krt_ref):
    cs = cs_ref[...]
    z, zb, a_v, q, ckv, kr = _project(x_ref[...], cs, g_pre_ref[...], w_in_ref[...],
                                      ones_ref[...], g_av_ref[...], g_q_ref[...], w_uq_ref[...], g_kv_ref[...])
    zb_ref[...] = zb
    av_ref[...] = a_v
    ckv_ref[...] = ckv
    kr_ref[...] = kr[:, :QK_ROPE]
    krt_ref[...] = kr.T[:QK_ROPE, :]
    s = w00_ref[...] * a_v + b0_ref[...]
    aout_ref[...] = (z[:, OFF_AU:OFF_AU + A_W] * s) * _silu(z[:, OFF_AG:OFF_AG + A_W])
    qlat_ref[...] = jnp.zeros(qlat_ref.shape, F32)
    qrope_ref[...] = jnp.zeros(qrope_ref.shape, F32)
    for hd in range(B_HEADS):
        qh = q[:, hd * Q_HEAD_W:(hd + 1) * Q_HEAD_W]
        qlat_ref[:, hd, :] = _dot(qh[:, :QK_NOPE].astype(BF16), w_ukt_ref[hd])
        qrope_ref[:, hd, :] = _rope128(qh[:, QK_NOPE:], cs)[:, :QK_ROPE]


def _sample_proj_call(x, cs, g_pre, w_in, ones_blk, g_av, w00, b0, g_q, w_uq, g_kv, w_ukt):
    n = x.shape[0]
    args = (x, cs, g_pre, w_in, ones_blk, g_av, w00, b0, g_q, w_uq, g_kv, w_ukt)
    shapes = ((n, IN_B), (n, A_W), (n, A_W), (n, DEC_ROWS, KV_LORA), (n, DEC_ROWS, QK_ROPE),
              (n, KV_LORA), (n, QK_ROPE), (QK_ROPE, n))
    return pl.pallas_call(
        _sample_proj_kernel,
        grid=(1,),
        in_specs=[_full(a.shape) for a in args],
        out_specs=[_full(s) for s in shapes],
        out_shape=[jax.ShapeDtypeStruct(s, F32) for s in shapes],
        compiler_params=pltpu.CompilerParams(dimension_semantics=("arbitrary",)),
        name="sample_proj",
    )(*args)


def _decode_kernel(pt_ref, qlat_ref, qrope_ref, cnew_ref, knew_ref, z_ref, mkt_ref, mvt_ref, ckv_hbm, krt_hbm,
                   o_ref, c_ref, cbuf, kbuf, sems, *, n_seq, n_regions):
    b = pl.program_id(0)
    n_sub = DEC_RP // DEC_SP

    def page_copies(seq, region, slot, sub):
        out = []
        for k in range(sub * DEC_SP, (sub + 1) * DEC_SP):
            page = pt_ref[seq, region * DEC_RP + k]
            out.append(pltpu.make_async_copy(ckv_hbm.at[page], cbuf.at[slot, k], sems.at[0, slot]))
            out.append(pltpu.make_async_copy(krt_hbm.at[page], kbuf.at[slot, k], sems.at[1, slot]))
        return out

    def ahead_of(r):
        seq = b + (r + DEC_AHEAD) // n_regions
        seq = jnp.where(seq < n_seq, seq, seq - n_seq)
        return seq, (r + DEC_AHEAD) % n_regions, lax.rem(b * n_regions + r + DEC_AHEAD, DEC_NBUF)

    @pl.when(b == 0)
    def _():
        for g in range(DEC_AHEAD):
            for sub in range(n_sub):
                for cp in page_copies(g // n_regions, g % n_regions, g % DEC_NBUF, sub):
                    cp.start()

    q_lat = qlat_ref[...]
    q_rope = qrope_ref[...]
    c_new = cnew_ref[...]
    k_new = knew_ref[...]
    s_new = (jnp.sum(q_lat * c_new, axis=-1, keepdims=True)
             + jnp.sum(q_rope * k_new, axis=-1, keepdims=True)) * SM_SCALE
    q_lat = q_lat.astype(BF16)
    q_rope = q_rope.astype(BF16)

    def mem_attention():
        row = lax.broadcasted_iota(jnp.int32, (SUBLANES, C_W), 0)
        own = _lane_head((SUBLANES, C_W), C_HD) == row
        cq = jnp.broadcast_to(z_ref[pl.ds(b, 1), OFF_CQC:OFF_CQC + C_W], (SUBLANES, C_W))
        p = _softmax_lanes(_dot(jnp.where(own, cq, 0.0).astype(BF16), mkt_ref[...].astype(BF16)) * MEM_SCALE)
        o = _dot_nt(p.astype(BF16), mvt_ref[...].astype(BF16))
        c_ref[...] = jnp.sum(jnp.where(own, o, 0.0), axis=0, keepdims=True)

    parts = []
    for r in range(n_regions):
        slot = lax.rem(b * n_regions + r, DEC_NBUF)
        for sub in range(n_sub):
            for cp in page_copies(b, r, slot, sub):
                cp.wait()
        nxt_seq, nxt_region, nxt_slot = ahead_of(r)
        if r == 0:
            mem_attention()
        for j in range(n_sub):
            ckv_f = cbuf[slot, pl.ds(j * DEC_SP, DEC_SP)].reshape(DEC_SP * PAGE, KV_LORA)
            ckv_j = ckv_f.astype(BF16)
            s_rope = jnp.concatenate(
                [_dot(q_rope, kbuf[slot, j * DEC_SP + k].astype(BF16)) for k in range(DEC_SP)], axis=1)
            s = (_dot(q_lat, ckv_f.T.astype(BF16)) + s_rope) * SM_SCALE
            m_j = jnp.max(s, axis=-1, keepdims=True)
            p = jnp.exp(s - m_j)
            parts.append((m_j, jnp.sum(p, axis=-1, keepdims=True), _dot(p.astype(BF16), ckv_j)))
        for sub in range(n_sub):
            for cp in page_copies(nxt_seq, nxt_region, nxt_slot, sub):
                cp.start()

    m = s_new
    for m_j, _, _ in parts:
        m = jnp.maximum(m, m_j)
    w_new = jnp.exp(s_new - m)
    l = w_new
    acc = w_new * c_new
    for m_j, l_j, acc_j in parts:
        w_j = jnp.exp(m_j - m)
        l = l + w_j * l_j
        acc = acc + w_j * acc_j
    o_ref[...] = acc / l

    @pl.when(b == n_seq - 1)
    def _():
        total = n_seq * n_regions
        for g in range(total, total + DEC_AHEAD):
            for sub in range(n_sub):
                for cp in page_copies(g // n_regions - n_seq, g % n_regions, g % DEC_NBUF, sub):
                    cp.wait()


def _decode_call(page_table, q_lat, q_rope, c_new, k_new, z, mem_kt, mem_vt, pool_ckv, pool_krt):
    n_seq, n_pages = page_table.shape
    assert n_pages % DEC_RP == 0 and DEC_RP % DEC_SP == 0
    n_regions = n_pages // DEC_RP
    assert DEC_AHEAD <= n_regions * n_seq
    grid_spec = pltpu.PrefetchScalarGridSpec(
        num_scalar_prefetch=1,
        grid=(n_seq,),
        in_specs=[
            pl.BlockSpec((None, DEC_ROWS, KV_LORA), lambda b, pt: (b, 0, 0)),
            pl.BlockSpec((None, DEC_ROWS, QK_ROPE), lambda b, pt: (b, 0, 0)),
            pl.BlockSpec((None, 1, KV_LORA), lambda b, pt: (b, 0, 0)),
            pl.BlockSpec((None, 1, QK_ROPE), lambda b, pt: (b, 0, 0)),
            pl.BlockSpec(z.shape, lambda b, pt: (0, 0)),
            pl.BlockSpec((None, C_W, N_MEM), lambda b, pt: (b, 0, 0)),
            pl.BlockSpec((None, C_W, N_MEM), lambda b, pt: (b, 0, 0)),
            pl.BlockSpec(memory_space=pl.ANY),
            pl.BlockSpec(memory_space=pl.ANY),
        ],
        out_specs=[
            pl.BlockSpec((None, DEC_ROWS, KV_LORA), lambda b, pt: (b, 0, 0)),
            pl.BlockSpec((None, 1, C_W), lambda b, pt: (b, 0, 0)),
        ],
        scratch_shapes=[
            pltpu.VMEM((DEC_NBUF, DEC_RP, PAGE, KV_LORA), F32),
            pltpu.VMEM((DEC_NBUF, DEC_RP, QK_ROPE, PAGE), F32),
            pltpu.SemaphoreType.DMA((2, DEC_NBUF)),
        ],
    )
    return pl.pallas_call(
        functools.partial(_decode_kernel, n_seq=n_seq, n_regions=n_regions),
        grid_spec=grid_spec,
        out_shape=[
            jax.ShapeDtypeStruct((n_seq, DEC_ROWS, KV_LORA), F32),
            jax.ShapeDtypeStruct((n_seq, 1, C_W), F32),
        ],
        compiler_params=pltpu.CompilerParams(
            dimension_semantics=("arbitrary",),
            vmem_limit_bytes=DECODE_VMEM_LIMIT),
        name="paged_decode",
    )(page_table, q_lat, q_rope, c_new, k_new, z, mem_kt, mem_vt, pool_ckv, pool_krt)


def _sample_tail_kernel(x_ref, z_ref, aout_ref, olat_ref, cout_ref, w_uv_ref, w_out_ref, g_post_ref, y_ref, mix_scr):
    z = z_ref[...]
    mix_scr[:, 0:A_W] = aout_ref[...].astype(BF16)
    for hd in range(B_HEADS):
        o = _dot(olat_ref[:, hd, :].astype(BF16), w_uv_ref[hd])
        b_g = z[:, OFF_BG + hd * V_HEAD:OFF_BG + (hd + 1) * V_HEAD]
        mix_scr[:, A_W + hd * V_HEAD:A_W + (hd + 1) * V_HEAD] = (o * _silu(b_g)).astype(BF16)
    mix_scr[:, A_W + B_W:] = (cout_ref[...] * _silu(z[:, OFF_CG:OFF_CG + C_W])).astype(BF16)
    y_ref[...] = _finish(x_ref[...], mix_scr[...], w_out_ref[...], g_post_ref[...])


def _sample_tail_call(x, z, a_out, o_lat, c_out, w_uv, w_out, g_post):
    n = x.shape[0]
    args = (x, z, a_out, o_lat, c_out, w_uv, w_out, g_post)
    return pl.pallas_call(
        _sample_tail_kernel,
        grid=(1,),
        in_specs=[_full(a.shape) for a in args],
        out_specs=_full((n, D_MODEL)),
        out_shape=jax.ShapeDtypeStruct((n, D_MODEL), F32),
        scratch_shapes=[pltpu.VMEM((n, D_MODEL), BF16)],
        compiler_params=pltpu.CompilerParams(dimension_semantics=("arbitrary",)),
        name="sample_tail",
    )(*args)


def _rope_table(pos):
    inv = ROPE_BASE ** (-jnp.arange(HALF, dtype=F32) / HALF)
    ang = pos[:, None] * inv[None, :]
    cos, sin = jnp.cos(ang), jnp.sin(ang)
    return jnp.concatenate([cos, cos, -sin, sin], axis=-1)


def kernel(x_prompt, x_sample, mem_prompt, cache_ckv, cache_krope, cache_mem_k, cache_mem_v, page_table,
           g_pre, w_in, g_av, w_s, b_s, g_q, w_uq, g_kv, w_uk, w_uv, g_mem, w_mem_k, w_mem_v, w_out, g_post):
    depth = g_pre.shape[0]
    assert depth == 1
    nb, seq, _ = x_prompt.shape
    n_seq, dec_seq, _ = x_sample.shape
    assert dec_seq == 1 and seq % TQ == 0
    past_len = page_table.shape[1] * PAGE
    l = 0

    row = lambda g: g.reshape(1, -1)
    w_in_ext, w_uq_ext, w_memkv = _weight_prep_call(
        jnp.transpose(w_in[l]),
        jnp.transpose(w_uq[l], (1, 2, 0)).reshape(B_HEADS * (QK_NOPE + QK_ROPE), Q_LORA),
        jnp.transpose(w_mem_k[l], (1, 2, 0)).reshape(C_W, D_MODEL),
        jnp.transpose(w_mem_v[l], (1, 2, 0)).reshape(C_W, D_MODEL))
    w_ukv = jnp.concatenate([w_uk[l].reshape(KV_LORA, B_W), w_uv[l].reshape(KV_LORA, B_W)], axis=1).astype(BF16)
    w_ukt = jnp.transpose(w_uk[l], (1, 2, 0)).astype(BF16)
    w_uv_h = jnp.transpose(w_uv[l], (1, 0, 2)).astype(BF16)
    w_out_b = w_out[l].astype(BF16)
    ones_blk = (jnp.arange(A_W)[:, None] // A_HD == jnp.arange(A_W)[None, :] // A_HD).astype(BF16)
    g_av_l = g_av[l].reshape(1, A_W)
    b_s_lanes = jnp.repeat(jnp.transpose(b_s[l]), A_HD, axis=1)
    w00 = jnp.repeat(w_s[l][:, 0, 0], A_HD).reshape(1, A_W)
    b0 = b_s_lanes[0:1]

    cs_p = _rope_table(jnp.arange(seq, dtype=F32))
    cs_s = jnp.broadcast_to(_rope_table(past_len + jnp.arange(dec_seq, dtype=F32)), (n_seq, 2 * QK_ROPE))

    y_p, ckv_p, krt_p, mkt_p, mvt_p = _prompt_call(
        x_prompt, cs_p, mem_prompt, row(g_mem[l]), w_memkv, row(g_pre[l]), w_in_ext, ones_blk, g_av_l,
        w_s[l], b_s_lanes, row(g_q[l]), w_uq_ext, row(g_kv[l]), w_ukv, w_out_b, row(g_post[l]))

    x_s = x_sample.reshape(n_seq, D_MODEL)
    z_s, av_s, aout_s, q_lat, q_rope, ckv_s, kr_s, krt_s = _sample_proj_call(
        x_s, cs_s, row(g_pre[l]), w_in_ext, ones_blk, g_av_l, w00, b0, row(g_q[l]), w_uq_ext, row(g_kv[l]), w_ukt)
    pool_krt = jnp.swapaxes(cache_krope[l], 1, 2)
    mem_kt = jnp.transpose(cache_mem_k[l], (0, 2, 3, 1)).reshape(n_seq, C_W, N_MEM)
    mem_vt = jnp.transpose(cache_mem_v[l], (0, 2, 3, 1)).reshape(n_seq, C_W, N_MEM)
    o_lat, c_out = _decode_call(page_table, q_lat, q_rope, ckv_s.reshape(n_seq, 1, KV_LORA),
                                kr_s.reshape(n_seq, 1, QK_ROPE), z_s, mem_kt, mem_vt, cache_ckv[l], pool_krt)
    y_s = _sample_tail_call(x_s, z_s, aout_s, o_lat, c_out.reshape(n_seq, C_W), w_uv_h, w_out_b, row(g_post[l]))

    n_pg = seq // PAGE
    return (
        y_p,
        y_s.reshape(n_seq, dec_seq, D_MODEL),
        ckv_p.reshape(depth, nb, n_pg, PAGE, KV_LORA),
        jnp.swapaxes(krt_p, -1, -2).reshape(depth, nb, n_pg, PAGE, QK_ROPE),
        jnp.transpose(mkt_p.reshape(depth, nb, C_HEADS, C_HD, N_MEM), (0, 1, 4, 2, 3)),
        jnp.transpose(mvt_p.reshape(depth, nb, C_HEADS, C_HD, N_MEM), (0, 1, 4, 2, 3)),
        ckv_s.reshape(depth, n_seq, dec_seq, KV_LORA),
        jnp.transpose(krt_s).reshape(depth, n_seq, dec_seq, QK_ROPE),
        av_s.reshape(depth, n_seq, dec_seq, A_HEADS, A_HD),
    )
```
